```python
import math
import jax
import jax.numpy as jnp
from jax import lax
import numpy as np

D_MODEL = 4096
BATCH = 2
SEQ = 8192
DEPTH = 2

CHUNK = 64
Q_BLOCK = 128
MIX_WIDTH = D_MODEL

SB_WIDTH = MIX_WIDTH // 2
SB_HEAD_DIM = 128
SB_HEADS = SB_WIDTH // SB_HEAD_DIM
RWKV_WIDTH = MIX_WIDTH - SB_WIDTH
RWKV_HEAD_DIM = 64
RWKV_HEADS = RWKV_WIDTH // RWKV_HEAD_DIM
LORA_W = 96
LORA_A = 96
LORA_G = 256
RWKV_FEAT = 3 * RWKV_WIDTH + LORA_W + LORA_A + LORA_G
RWKV_DECAY_SCALE = math.exp(-0.5)
RWKV_GN_EPS = 64e-5
IN_EVEN = 3 * SB_WIDTH + RWKV_FEAT

DIFF_WIDTH = MIX_WIDTH // 2
DIFF_HEAD_DIM = 128
DIFF_HEADS = DIFF_WIDTH // (2 * DIFF_HEAD_DIM)
GLA_WIDTH = MIX_WIDTH - DIFF_WIDTH
GLA_HEADS = 4
GLA_QK = GLA_WIDTH // 2
GLA_DK = GLA_QK // GLA_HEADS
GLA_DV = GLA_WIDTH // GLA_HEADS
GLA_GATE_RANK = 16
GLA_GATE_NORM = 16.0
IN_ODD = 3 * DIFF_WIDTH + 2 * GLA_QK + GLA_WIDTH + GLA_GATE_RANK + GLA_WIDTH

N_GROUPS = 4
EXPERTS_PER_GROUP = 8
N_EXPERTS = N_GROUPS * EXPERTS_PER_GROUP
TOP_K = 2
D_EXPERT = 3 * D_MODEL // 16
MOE_BLOCK = 128

DEEPNORM_ALPHA = (2 * DEPTH) ** 0.25
DEEPNORM_BETA = (8 * DEPTH) ** -0.25
LN_EPS = 1e-5

kernel_name = "hybrid_sb_rwkv7_diff_gla_hmoe"


def _offsets(widths):
    out, acc = [], 0
    for w in widths:
        acc += w
        out.append(acc)
    return out


def layer_norm(x, g, b):
    xf = x.astype(jnp.float32)
    mu = jnp.mean(xf, -1, keepdims=True)
    var = jnp.mean(jnp.square(xf - mu), -1, keepdims=True)
    return ((xf - mu) * lax.rsqrt(var + LN_EPS) * g + b).astype(x.dtype)


def rms_norm(x, w, eps=1e-5):
    xf = x.astype(jnp.float32)
    return xf * lax.rsqrt(jnp.mean(jnp.square(xf), -1, keepdims=True) + eps) * w


def token_shift(f, mu):
    prev = jnp.pad(f, ((0, 0), (1, 0), (0, 0)))[:, :-1]
    return f + (prev - f) * mu


def diff_lambda_init(layer):
    return 0.8 - 0.6 * math.exp(-0.3 * layer)


def stick_breaking_attention(q, k, v):
    B, H, S, d = q.shape
    scale = d ** -0.5
    qf, kf, vf = (t.astype(jnp.float32) for t in (q, k, v))
    key_pos = jnp.arange(S)

    def block(i):
        start = i * Q_BLOCK
        qb = lax.dynamic_slice_in_dim(qf, start, Q_BLOCK, axis=2)
        z = jnp.einsum('bhqd,bhkd->bhqk', qb, kf) * scale
        strict = key_pos[None, :] < (start + jnp.arange(Q_BLOCK))[:, None]
        log_keep = jnp.where(strict, jax.nn.log_sigmoid(-z), 0.0)
        log_between = lax.cumsum(log_keep, axis=3, reverse=True) - log_keep
        weight = jnp.where(strict, jnp.exp(jax.nn.log_sigmoid(z) + log_between), 0.0)
        return jnp.einsum('bhqk,bhkd->bhqd', weight, vf)

    out = lax.map(block, jnp.arange(S // Q_BLOCK))
    return out.transpose(1, 2, 0, 3, 4).reshape(B, H, S, d)


def rwkv7_recurrence(r, w, k, v, a, b):
    B, S, H, N = r.shape
    xs = tuple(jnp.moveaxis(t, 1, 0) for t in (r, w, k, v, a, b))

    def step(state, inp):
        rt, wt, kt, vt, at, bt = inp
        sa = jnp.einsum('bhvk,bhk->bhv', state, at)
        state = (state * wt[:, :, None, :] + sa[..., None] * bt[:, :, None, :]
                 + vt[..., None] * kt[:, :, None, :])
        return state, jnp.einsum('bhvk,bhk->bhv', state, rt)

    _, y = lax.scan(step, jnp.zeros((B, H, N, N), jnp.float32), xs)
    return jnp.moveaxis(y, 0, 1)


def rwkv7_mixer(feat, w0, w2, a0, a2, g2, k_k, k_a, r_k, gn_w, gn_b):
    B, S, _ = feat.shape
    f = feat.astype(jnp.float32)
    r, k, v, cw, ca, cg = jnp.split(
        f, _offsets([RWKV_WIDTH, RWKV_WIDTH, RWKV_WIDTH, LORA_W, LORA_A]), axis=-1)
    w = jnp.exp(-RWKV_DECAY_SCALE * jax.nn.sigmoid(w0 + jnp.tanh(cw) @ w2))
    a = jax.nn.sigmoid(a0 + ca @ a2)
    g = jax.nn.sigmoid(cg) @ g2
    hd = lambda t: t.reshape(B, S, RWKV_HEADS, RWKV_HEAD_DIM)
    kk = hd(k * k_k)
    kk = kk * lax.rsqrt(jnp.maximum(jnp.sum(jnp.square(kk), -1, keepdims=True), 1e-24))
    k = k * (1.0 + (a - 1.0) * k_a)
    r, w, k, v, a = hd(r), hd(w), hd(k), hd(v), hd(a)
    y = rwkv7_recurrence(r, w, k, v, -kk, kk * a)
    mu = jnp.mean(y, -1, keepdims=True)
    var = jnp.mean(jnp.square(y - mu), -1, keepdims=True)
    y = ((y - mu) * lax.rsqrt(var + RWKV_GN_EPS)).reshape(B, S, RWKV_WIDTH) * gn_w + gn_b
    bonus = jnp.sum(r * k * r_k, -1, keepdims=True) * v
    return (y + bonus.reshape(B, S, RWKV_WIDTH)) * g


def sb_rwkv_mixer(x, w_in, shift_mu, w0, w2, a0, a2, g2, k_k, k_a, r_k, gn_w, gn_b, w_out):
    B, S, _ = x.shape
    proj = x @ w_in
    q, k, v, rw = jnp.split(proj, _offsets([SB_WIDTH, SB_WIDTH, SB_WIDTH]), axis=-1)
    heads = lambda t: t.reshape(B, S, SB_HEADS, SB_HEAD_DIM).transpose(0, 2, 1, 3)
    o_sb = stick_breaking_attention(heads(q), heads(k), heads(v))
    o_sb = o_sb.transpose(0, 2, 1, 3).reshape(B, S, SB_WIDTH)
    o_rw = rwkv7_mixer(token_shift(rw, shift_mu), w0, w2, a0, a2, g2, k_k, k_a, r_k, gn_w, gn_b)
    return jnp.concatenate([o_sb, o_rw], axis=-1).astype(x.dtype) @ w_out


def differential_attention(q, k, v, lam):
    B, H, _, S, d = q.shape
    scale = d ** -0.5
    qf, kf, vf = (t.astype(jnp.float32) for t in (q, k, v))
    key_chunk = jnp.arange(S) // CHUNK
    neg = jnp.finfo(jnp.float32).min

    def block(i):
        start = i * Q_BLOCK
        qb = lax.dynamic_slice_in_dim(qf, start, Q_BLOCK, axis=3)
        s = jnp.einsum('bhiqd,bhikd->bhiqk', qb, kf) * scale
        allowed = key_chunk[None, :] <= ((start + jnp.arange(Q_BLOCK)) // CHUNK)[:, None]
        p = jax.nn.softmax(jnp.where(allowed, s, neg), axis=-1)
        return jnp.einsum('bhqk,bhkd->bhqd', p[:, :, 0] - lam * p[:, :, 1], vf)

    out = lax.map(block, jnp.arange(S // Q_BLOCK))
    return out.transpose(1, 2, 0, 3, 4).reshape(B, H, S, 2 * d)


def gla_chunked(q, k, v, log_a):
    B, H, S, dk = q.shape
    dv = v.shape[-1]
    N = S // CHUNK
    rs = lambda t: t.astype(jnp.float32).reshape(B, H, N, CHUNK, t.shape[-1])
    qc, kc, vc, gc = rs(q), rs(k), rs(v), rs(log_a)
    b = jnp.cumsum(gc, axis=3)
    b_last = b[:, :, :, -1:]
    q_dec = qc * (dk ** -0.5) * jnp.exp(b)
    k_inv = kc * jnp.exp(-b)
    k_end = kc * jnp.exp(b_last - b)
    causal = jnp.tril(jnp.ones((CHUNK, CHUNK), bool))
    scores = jnp.where(causal, jnp.einsum('bhncd,bhnsd->bhncs', q_dec, k_inv), 0.0)
    o_intra = jnp.einsum('bhncs,bhnse->bhnce', scores, vc)

    def step(state, inp):
        qd, ke, vv, decay = inp
        o = jnp.einsum('bhcd,bhde->bhce', qd, state)
        state = state * decay[..., None] + jnp.einsum('bhcd,bhce->bhde', ke, vv)
        return state, o

    xs = (jnp.moveaxis(q_dec, 2, 0), jnp.moveaxis(k_end, 2, 0), jnp.moveaxis(vc, 2, 0),
          jnp.moveaxis(jnp.exp(b_last[:, :, :, 0]), 2, 0))
    _, o_inter = lax.scan(step, jnp.zeros((B, H, dk, dv), jnp.float32), xs)
    return (o_intra + jnp.moveaxis(o_inter, 0, 2)).reshape(B, H, S, dv)


def diff_gla_mixer(x, w_in, lq1, lk1, lq2, lk2, subln_w, gate_w2, gate_b, gla_norm_w, w_out,
                   lam_init):
    B, S, _ = x.shape
    proj = x @ w_in
    qc, kc, vc, qg, kg, vg, cg, og = jnp.split(
        proj, _offsets([DIFF_WIDTH, DIFF_WIDTH, DIFF_WIDTH, GLA_QK, GLA_QK, GLA_WIDTH,
                        GLA_GATE_RANK]), axis=-1)
    q = qc.reshape(B, S, DIFF_HEADS, 2, DIFF_HEAD_DIM).transpose(0, 2, 3, 1, 4)
    k = kc.reshape(B, S, DIFF_HEADS, 2, DIFF_HEAD_DIM).transpose(0, 2, 3, 1, 4)
    v = vc.reshape(B, S, DIFF_HEADS, 2 * DIFF_HEAD_DIM).transpose(0, 2, 1, 3)
    f32 = lambda t: t.astype(jnp.float32)
    lam = (jnp.exp(jnp.sum(f32(lq1) * f32(lk1))) - jnp.exp(jnp.sum(f32(lq2) * f32(lk2)))
           + lam_init)
    o_diff = differential_attention(q, k, v, lam).transpose(0, 2, 1, 3)
    o_diff = (rms_norm(o_diff, subln_w) * (1.0 - lam_init)).reshape(B, S, DIFF_WIDTH)
    log_a = jax.nn.log_sigmoid(f32(cg @ gate_w2 + gate_b)) / GLA_GATE_NORM
    heads = lambda t, dh: t.reshape(B, S, GLA_HEADS, dh).transpose(0, 2, 1, 3)
    o_gla = gla_chunked(heads(qg, GLA_DK), heads(kg, GLA_DK), heads(vg, GLA_DV),
                        heads(log_a, GLA_DK)).transpose(0, 2, 1, 3)
    o_gla = rms_norm(o_gla, gla_norm_w) * jax.nn.silu(f32(og).reshape(B, S, GLA_HEADS, GLA_DV))
    mix = jnp.concatenate([o_diff, o_gla.reshape(B, S, GLA_WIDTH)], axis=-1)
    return mix.astype(x.dtype) @ w_out


def hier_moe(x, router_g, router_g_b, router_e, router_e_b, w_gate, w_up, w_down):
    B, S, D = x.shape
    T = B * S
    x2 = x.reshape(T, D)
    xf = x2.astype(jnp.float32)
    g_logits = xf @ router_g.astype(jnp.float32) + router_g_b
    g_prob = jax.nn.softmax(g_logits, axis=-1)
    g_sel = jnp.argmax(g_logits, axis=-1)
    g_w = jnp.take_along_axis(g_prob, g_sel[:, None], axis=-1)[:, 0]
    e_logits = (xf @ router_e.astype(jnp.float32) + router_e_b).reshape(T, N_GROUPS,
                                                                       EXPERTS_PER_GROUP)
    e_in_group = jnp.take_along_axis(e_logits, g_sel[:, None, None], axis=1)[:, 0]
    top_vals, top_idx = lax.top_k(e_in_group, TOP_K)
    gates = jax.nn.softmax(top_vals, axis=-1) * g_w[:, None]
    expert_id = (g_sel[:, None] * EXPERTS_PER_GROUP + top_idx).astype(jnp.int32)

    A = T * TOP_K
    flat_e = expert_id.reshape(A)
    flat_w = gates.reshape(A)
    flat_tok = jnp.repeat(jnp.arange(T, dtype=jnp.int32), TOP_K)
    order = jnp.argsort(flat_e)
    e_sorted = flat_e[order]
    counts = jnp.zeros((N_EXPERTS,), jnp.int32).at[flat_e].add(1)
    padded = (counts + MOE_BLOCK - 1) // MOE_BLOCK * MOE_BLOCK
    pad_end = jnp.cumsum(padded)
    pad_start = pad_end - padded
    start = jnp.cumsum(counts) - counts
    dest = pad_start[e_sorted] + (jnp.arange(A, dtype=jnp.int32) - start[e_sorted])
    NB = A // MOE_BLOCK + N_EXPERTS
    slot_tok = jnp.full((NB * MOE_BLOCK,), T, jnp.int32).at[dest].set(flat_tok[order])
    slot_w = jnp.zeros((NB * MOE_BLOCK,), jnp.float32).at[dest].set(flat_w[order])
    block_e = jnp.minimum(
        jnp.searchsorted(pad_end, jnp.arange(NB, dtype=jnp.int32) * MOE_BLOCK, side='right'),
        N_EXPERTS - 1)
    x_pad = jnp.concatenate([x2, jnp.zeros((1, D), x2.dtype)], axis=0)

    def run_block(args):
        tok, e = args
        xb = x_pad[tok]
        h = jax.nn.silu(xb @ w_gate[e]) * (xb @ w_up[e])
        return h @ w_down[e]

    yb = lax.map(run_block, (slot_tok.reshape(NB, MOE_BLOCK), block_e))
    y = jnp.zeros((T + 1, D), jnp.float32).at[slot_tok].add(
        yb.reshape(NB * MOE_BLOCK, D).astype(jnp.float32) * slot_w[:, None])
    return y[:T].reshape(B, S, D).astype(x.dtype)


def setup_inputs(seed: int = 0) -> dict:
    key = jax.random.key(seed)
    keys = jax.random.split(key, 64)
    count = [0]

    def nk():
        k = keys[count[0]]
        count[0] += 1
        return k

    def normal(shape, scale):
        return jax.random.normal(nk(), shape, jnp.float32) * scale

    def gain(n):
        return 1.0 + normal((n,), 0.02)

    def tail(p, pre):
        p[pre + 'ln1_g'] = gain(D_MODEL)
        p[pre + 'ln1_b'] = normal((D_MODEL,), 0.02)
        p[pre + 'router_g'] = normal((D_MODEL, N_GROUPS), D_MODEL ** -0.5)
        p[pre + 'router_g_b'] = normal((N_GROUPS,), 0.01)
        p[pre + 'router_e'] = normal((D_MODEL, N_EXPERTS), D_MODEL ** -0.5)
        p[pre + 'router_e_b'] = normal((N_EXPERTS,), 0.01)
        p[pre + 'w_gate'] = normal((N_EXPERTS, D_MODEL, D_EXPERT), D_MODEL ** -0.5)
        p[pre + 'w_up'] = normal((N_EXPERTS, D_MODEL, D_EXPERT), D_MODEL ** -0.5)
        p[pre + 'w_down'] = normal((N_EXPERTS, D_EXPERT, D_MODEL),
                                   DEEPNORM_BETA * D_EXPERT ** -0.5)
        p[pre + 'ln2_g'] = gain(D_MODEL)
        p[pre + 'ln2_b'] = normal((D_MODEL,), 0.02)

    p = {}
    p['x'] = normal((BATCH, SEQ, D_MODEL), 1.0)
    p['l0_w_in'] = normal((D_MODEL, IN_EVEN), D_MODEL ** -0.5)
    p['l0_shift_mu'] = jax.random.uniform(nk(), (RWKV_FEAT,), jnp.float32, minval=0.2, maxval=0.8)
    p['l0_w0'] = normal((RWKV_WIDTH,), 0.5)
    p['l0_w2'] = normal((LORA_W, RWKV_WIDTH), LORA_W ** -0.5)
    p['l0_a0'] = normal((RWKV_WIDTH,), 0.5)
    p['l0_a2'] = normal((LORA_A, RWKV_WIDTH), LORA_A ** -0.5)
    p['l0_g2'] = normal((LORA_G, RWKV_WIDTH), LORA_G ** -0.5)
    p['l0_k_k'] = 0.85 + normal((RWKV_WIDTH,), 0.05)
    p['l0_k_a'] = 1.0 + normal((RWKV_WIDTH,), 0.05)
    p['l0_r_k'] = normal((RWKV_HEADS, RWKV_HEAD_DIM), 0.1)
    p['l0_gn_w'] = gain(RWKV_WIDTH)
    p['l0_gn_b'] = normal((RWKV_WIDTH,), 0.02)
    p['l0_w_out'] = normal((MIX_WIDTH, D_MODEL), DEEPNORM_BETA * MIX_WIDTH ** -0.5)
    tail(p, 'l0_')
    p['l1_w_in'] = normal((D_MODEL, IN_ODD), D_MODEL ** -0.5)
    p['l1_lq1'] = normal((DIFF_HEAD_DIM,), 0.1)
    p['l1_lk1'] = normal((DIFF_HEAD_DIM,), 0.1)
    p['l1_lq2'] = normal((DIFF_HEAD_DIM,), 0.1)
    p['l1_lk2'] = normal((DIFF_HEAD_DIM,), 0.1)
    p['l1_subln_w'] = gain(2 * DIFF_HEAD_DIM)
    p['l1_gate_w2'] = normal((GLA_GATE_RANK, GLA_QK), GLA_GATE_RANK ** -0.5)
    p['l1_gate_b'] = normal((GLA_QK,), 0.1)
    p['l1_gla_norm_w'] = gain(GLA_DV)
    p['l1_w_out'] = normal((MIX_WIDTH, D_MODEL), DEEPNORM_BETA * MIX_WIDTH ** -0.5)
    tail(p, 'l1_')
    return p


def reference(x,
              l0_w_in, l0_shift_mu, l0_w0, l0_w2, l0_a0, l0_a2, l0_g2, l0_k_k, l0_k_a, l0_r_k,
              l0_gn_w, l0_gn_b, l0_w_out,
              l0_ln1_g, l0_ln1_b, l0_router_g, l0_router_g_b, l0_router_e, l0_router_e_b,
              l0_w_gate, l0_w_up, l0_w_down, l0_ln2_g, l0_ln2_b,
              l1_w_in, l1_lq1, l1_lk1, l1_lq2, l1_lk2, l1_subln_w, l1_gate_w2, l1_gate_b,
              l1_gla_norm_w, l1_w_out,
              l1_ln1_g, l1_ln1_b, l1_router_g, l1_router_g_b, l1_router_e, l1_router_e_b,
              l1_w_gate, l1_w_up, l1_w_down, l1_ln2_g, l1_ln2_b):
    layers = (
        ((l0_w_in, l0_shift_mu, l0_w0, l0_w2, l0_a0, l0_a2, l0_g2, l0_k_k, l0_k_a, l0_r_k,
          l0_gn_w, l0_gn_b, l0_w_out),
         (l0_ln1_g, l0_ln1_b),
         (l0_router_g, l0_router_g_b, l0_router_e, l0_router_e_b, l0_w_gate, l0_w_up, l0_w_down),
         (l0_ln2_g, l0_ln2_b)),
        ((l1_w_in, l1_lq1, l1_lk1, l1_lq2, l1_lk2, l1_subln_w, l1_gate_w2, l1_gate_b,
          l1_gla_norm_w, l1_w_out),
         (l1_ln1_g, l1_ln1_b),
         (l1_router_g, l1_router_g_b, l1_router_e, l1_router_e_b, l1_w_gate, l1_w_up, l1_w_down),
         (l1_ln2_g, l1_ln2_b)),
    )
    h = x
    for layer in range(DEPTH):
        mix_p, ln1, moe_p, ln2 = layers[layer]
        if layer % 2 == 0:
            mixed = sb_rwkv_mixer(h, *mix_p)
        else:
            mixed = diff_gla_mixer(h, *mix_p, diff_lambda_init(layer))
        h = layer_norm(DEEPNORM_ALPHA * h + mixed, *ln1)
        h = layer_norm(DEEPNORM_ALPHA * h + hier_moe(h, *moe_p), *ln2)
    return h
```

```python
import functools
import math

import jax
import jax.numpy as jnp
from jax import lax
from jax.experimental import pallas as pl
from jax.experimental.pallas import tpu as pltpu

F32 = jnp.float32
BF16 = jnp.bfloat16
I32 = jnp.int32

D_MODEL = 4096
DEPTH = 2
CHUNK = 64
SB_WIDTH = 2048
SB_HEAD_DIM = 128
SB_HEADS = SB_WIDTH // SB_HEAD_DIM
RWKV_WIDTH = 2048
RWKV_HEAD_DIM = 64
LORA_W = 96
LORA_A = 96
LORA_G = 256
RWKV_DECAY_SCALE = math.exp(-0.5)
RWKV_GN_EPS = 64e-5
DIFF_WIDTH = 2048
DIFF_HEAD_DIM = 128
DIFF_HEADS = DIFF_WIDTH // (2 * DIFF_HEAD_DIM)
GLA_WIDTH = 2048
GLA_HEADS = 4
GLA_QK = GLA_WIDTH // 2
GLA_DK = GLA_QK // GLA_HEADS
GLA_DV = GLA_WIDTH // GLA_HEADS
GLA_GATE_RANK = 16
GLA_GATE_NORM = 16.0
N_GROUPS = 4
EXPERTS_PER_GROUP = 8
N_EXPERTS = N_GROUPS * EXPERTS_PER_GROUP
TOP_K = 2
D_EXPERT = 768
DEEPNORM_ALPHA = (2 * DEPTH) ** 0.25
LN_EPS = 1e-5

LANES = 128
V7X_VMEM_LIMIT = 56 * 1024 * 1024

HIGHEST = lax.Precision.HIGHEST


def _cparams(semantics, vmem=V7X_VMEM_LIMIT):
    return pltpu.CompilerParams(dimension_semantics=semantics, vmem_limit_bytes=vmem)


def _dot(a, b, dims=(((1,), (0,)), ((), ()))):
    return lax.dot_general(a.astype(BF16), b.astype(BF16), dims, preferred_element_type=F32)


def _dot32(a, b, dims=(((1,), (0,)), ((), ()))):
    return lax.dot_general(a.astype(F32), b.astype(F32), dims, precision=HIGHEST,
                           preferred_element_type=F32)


_NT = (((1,), (1,)), ((), ()))
_TN = (((0,), (0,)), ((), ()))


def _mm_kernel(x_ref, w_ref, o_ref, *, exact):
    if exact:
        acc = _dot32(x_ref[...], w_ref[...])
    else:
        acc = _dot(x_ref[...], w_ref[...])
    o_ref[...] = acc.astype(o_ref.dtype)


def matmul(x, w, *, out_dtype, tm, tn, exact=False):
    m, k = x.shape
    n = w.shape[1]
    assert m % tm == 0 and n % tn == 0, (x.shape, w.shape, tm, tn)
    return pl.pallas_call(
        functools.partial(_mm_kernel, exact=exact),
        out_shape=jax.ShapeDtypeStruct((m, n), out_dtype),
        grid=(m // tm, n // tn),
        in_specs=[pl.BlockSpec((tm, k), lambda i, j: (i, 0)),
                  pl.BlockSpec((k, tn), lambda i, j: (0, j))],
        out_specs=pl.BlockSpec((tm, tn), lambda i, j: (i, j)),
        compiler_params=_cparams(("parallel", "arbitrary")),
        name="proj",
    )(x, w)


def _layer_norm_rows(x, g, b):
    mu = jnp.mean(x, axis=-1, keepdims=True)
    xc = x - mu
    var = jnp.mean(xc * xc, axis=-1, keepdims=True)
    return xc * lax.rsqrt(var + LN_EPS) * g + b


def _ln_router_kernel(h_ref, m_ref, g_ref, b_ref, r_ref, rb_ref, of_ref, ob_ref, lg_ref):
    y = _layer_norm_rows(DEEPNORM_ALPHA * h_ref[...] + m_ref[...], g_ref[...], b_ref[...])
    of_ref[...] = y
    ob_ref[...] = y.astype(BF16)
    lg_ref[...] = _dot32(y, r_ref[...]) + rb_ref[...]


def ln_router(h, mixed, g, b, router_w, router_b, *, tm=256):
    t, d = h.shape
    nr = router_w.shape[1]
    row = lambda i: (i, 0)
    fixed = lambda i: (0, 0)
    return pl.pallas_call(
        _ln_router_kernel,
        out_shape=(jax.ShapeDtypeStruct((t, d), F32), jax.ShapeDtypeStruct((t, d), BF16),
                   jax.ShapeDtypeStruct((t, nr), F32)),
        grid=(t // tm,),
        in_specs=[pl.BlockSpec((tm, d), row), pl.BlockSpec((tm, d), row),
                  pl.BlockSpec((1, d), fixed), pl.BlockSpec((1, d), fixed),
                  pl.BlockSpec((d, nr), fixed), pl.BlockSpec((1, nr), fixed)],
        out_specs=(pl.BlockSpec((tm, d), row), pl.BlockSpec((tm, d), row),
                   pl.BlockSpec((tm, nr), row)),
        compiler_params=_cparams(("parallel",)),
        name="ln_router",
    )(h, mixed, g.reshape(1, d), b.reshape(1, d), router_w, router_b.reshape(1, nr))


MOE_ROWS = 512
MOE_ESPLIT = 3


def _moe_ffn_kernel(tok_ref, be_ref, nb_ref, h_hbm, wg_ref, wu_ref, wd_ref, o_ref,
                    xrows, xb16, sem, *, rows, nsplit):
    i = pl.program_id(0)
    j = pl.program_id(1)

    def row_copy(r, t):
        return pltpu.make_async_copy(h_hbm.at[pl.ds(t, 1), :], xrows.at[pl.ds(r, 1), :], sem)

    @pl.when(i < nb_ref[0])
    def _():
        @pl.when(j == 0)
        def _():
            def issue(r, c):
                row_copy(r, tok_ref[i * rows + r]).start()
                return c
            lax.fori_loop(0, rows, issue, 0)

            def drain(r, c):
                row_copy(r, 0).wait()
                return c
            lax.fori_loop(0, rows, drain, 0)
            xb16[...] = xrows[...].astype(BF16)

        x = xb16[...]
        hg = _dot(x, wg_ref[...])
        hu = _dot(x, wu_ref[...])
        act = hg * jax.nn.sigmoid(hg) * hu
        part = _dot(act, wd_ref[...])

        @pl.when(j == 0)
        def _():
            o_ref[...] = part

        @pl.when(j > 0)
        def _():
            o_ref[...] += part

    @pl.when((i >= nb_ref[0]) & (j == 0))
    def _():
        o_ref[...] = jnp.zeros_like(o_ref)


def moe_ffn(h, slot_tok, block_e, nb_used, wg, wu, wd):
    t, d = h.shape
    rows = MOE_ROWS
    nblk = slot_tok.shape[0] // rows
    de = wg.shape[2]
    ce = de // MOE_ESPLIT

    def jj(i, j, nb):
        return jnp.where(i < nb[0], j, MOE_ESPLIT - 1)

    grid_spec = pltpu.PrefetchScalarGridSpec(
        num_scalar_prefetch=3,
        grid=(nblk, MOE_ESPLIT),
        in_specs=[
            pl.BlockSpec(memory_space=pl.ANY),
            pl.BlockSpec((None, d, ce), lambda i, j, tok, be, nb: (be[i], 0, jj(i, j, nb))),
            pl.BlockSpec((None, d, ce), lambda i, j, tok, be, nb: (be[i], 0, jj(i, j, nb))),
            pl.BlockSpec((None, ce, d), lambda i, j, tok, be, nb: (be[i], jj(i, j, nb), 0)),
        ],
        out_specs=pl.BlockSpec((rows, d), lambda i, j, tok, be, nb: (i, 0)),
        scratch_shapes=[pltpu.VMEM((rows, d), F32), pltpu.VMEM((rows, d), BF16),
                        pltpu.SemaphoreType.DMA(())],
    )
    return pl.pallas_call(
        functools.partial(_moe_ffn_kernel, rows=rows, nsplit=MOE_ESPLIT),
        out_shape=jax.ShapeDtypeStruct((nblk * rows, d), F32),
        grid_spec=grid_spec,
        compiler_params=_cparams(("arbitrary", "arbitrary")),
        name="moe_ffn",
    )(slot_tok, block_e, nb_used, h, wg, wu, wd)


def _combine_ln_kernel(pos_ref, h_ref, gate_ref, yb_hbm, g_ref, b_ref, of_ref, ob_ref,
                       ybuf, sem, *, tm):
    i = pl.program_id(0)

    def row_copy(k, r, p):
        return pltpu.make_async_copy(yb_hbm.at[pl.ds(p, 1), :], ybuf.at[k, pl.ds(r, 1), :], sem)

    def issue(r, c):
        a = (i * tm + r) * TOP_K
        for k in range(TOP_K):
            row_copy(k, r, pos_ref[a + k]).start()
        return c
    lax.fori_loop(0, tm, issue, 0)

    def drain(r, c):
        for k in range(TOP_K):
            row_copy(k, r, 0).wait()
        return c
    lax.fori_loop(0, tm, drain, 0)

    gate = gate_ref[...]
    y = ybuf[0] * gate[:, 0:1]
    for k in range(1, TOP_K):
        y = y + ybuf[k] * gate[:, k:k + 1]
    out = _layer_norm_rows(DEEPNORM_ALPHA * h_ref[...] + y, g_ref[...], b_ref[...])
    of_ref[...] = out
    ob_ref[...] = out.astype(BF16)


def combine_ln(h, gates, pos, yb, g, b, *, tm=256):
    t, d = h.shape
    grid_spec = pltpu.PrefetchScalarGridSpec(
        num_scalar_prefetch=1,
        grid=(t // tm,),
        in_specs=[
            pl.BlockSpec((tm, d), lambda i, pos: (i, 0)),
            pl.BlockSpec((tm, TOP_K), lambda i, pos: (i, 0)),
            pl.BlockSpec(memory_space=pl.ANY),
            pl.BlockSpec((1, d), lambda i, pos: (0, 0)),
            pl.BlockSpec((1, d), lambda i, pos: (0, 0)),
        ],
        out_specs=(pl.BlockSpec((tm, d), lambda i, pos: (i, 0)),
                   pl.BlockSpec((tm, d), lambda i, pos: (i, 0))),
        scratch_shapes=[pltpu.VMEM((TOP_K, tm, d), F32), pltpu.SemaphoreType.DMA(())],
    )
    return pl.pallas_call(
        functools.partial(_combine_ln_kernel, tm=tm),
        out_shape=(jax.ShapeDtypeStruct((t, d), F32), jax.ShapeDtypeStruct((t, d), BF16)),
        grid_spec=grid_spec,
        compiler_params=_cparams(("arbitrary",)),
        name="combine_ln",
    )(pos.reshape(-1), h, gates, yb, g.reshape(1, d), b.reshape(1, d))


def _routing_tables(logits, t):
    g_logits = logits[:, :N_GROUPS]
    g_prob = jax.nn.softmax(g_logits, axis=-1)
    g_sel = jnp.argmax(g_logits, axis=-1)
    g_w = jnp.take_along_axis(g_prob, g_sel[:, None], axis=-1)[:, 0]
    e_logits = logits[:, N_GROUPS:N_GROUPS + N_EXPERTS].reshape(t, N_GROUPS, EXPERTS_PER_GROUP)
    e_in_group = jnp.take_along_axis(e_logits, g_sel[:, None, None], axis=1)[:, 0]
    top_vals, top_idx = lax.top_k(e_in_group, TOP_K)
    gates = jax.nn.softmax(top_vals, axis=-1) * g_w[:, None]
    expert_id = (g_sel[:, None] * EXPERTS_PER_GROUP + top_idx).astype(I32)

    a = t * TOP_K
    rows = MOE_ROWS
    flat_e = expert_id.reshape(a)
    flat_tok = jnp.repeat(jnp.arange(t, dtype=I32), TOP_K)
    order = jnp.argsort(flat_e)
    e_sorted = flat_e[order]
    counts = jnp.zeros((N_EXPERTS,), I32).at[flat_e].add(1)
    padded = (counts + rows - 1) // rows * rows
    pad_end = jnp.cumsum(padded)
    pad_start = pad_end - padded
    start = jnp.cumsum(counts) - counts
    dest = pad_start[e_sorted] + (jnp.arange(a, dtype=I32) - start[e_sorted])
    nblk = a // rows + N_EXPERTS
    slot_tok = jnp.zeros((nblk * rows,), I32).at[dest].set(flat_tok[order])
    pos = jnp.zeros((a,), I32).at[order].set(dest.astype(I32))
    block_e = jnp.minimum(
        jnp.searchsorted(pad_end, jnp.arange(nblk, dtype=I32) * rows, side='right'),
        N_EXPERTS - 1).astype(I32)
    nb_used = (pad_end[-1] // rows).astype(I32).reshape(1)
    return gates.astype(F32), pos.reshape(t, TOP_K), slot_tok, block_e, nb_used


def hier_moe_ln(h_f32, logits, wg, wu, wd, ln_g, ln_b):
    t = h_f32.shape[0]
    gates, pos, slot_tok, block_e, nb_used = _routing_tables(logits, t)
    yb = moe_ffn(h_f32, slot_tok, block_e, nb_used, wg, wu, wd)
    return combine_ln(h_f32, gates, pos, yb, ln_g, ln_b)


ATT_TQ = 256
ATT_TK = 256
NEG_BIG = -1e30


def _diff_attn_kernel(lam_ref, q1_ref, q2_ref, k1_ref, k2_ref, v_ref, sw_ref, o_ref,
                      m_ref, l_ref, acc_ref, *, tq, tk, scale, out_scale):
    i = pl.program_id(2)
    qs = [(q1_ref[...].astype(F32) * scale).astype(BF16),
          (q2_ref[...].astype(F32) * scale).astype(BF16)]
    k_refs = [k1_ref, k2_ref]
    m_ref[...] = jnp.full(m_ref.shape, NEG_BIG, F32)
    l_ref[...] = jnp.zeros(l_ref.shape, F32)
    acc_ref[...] = jnp.zeros(acc_ref.shape, F32)
    row_chunk = lax.broadcasted_iota(I32, (tq, tk), 0) // CHUNK
    col_chunk = lax.broadcasted_iota(I32, (tq, tk), 1) // CHUNK
    diag_ok = col_chunk <= row_chunk

    def step(j, carry):
        start = pl.multiple_of(j * tk, tk)
        vb = v_ref[pl.ds(start, tk), :]
        allowed = diag_ok | (j < i)
        for br in range(2):
            kb = k_refs[br][pl.ds(start, tk), :]
            s = lax.dot_general(qs[br], kb, _NT, preferred_element_type=F32)
            s = jnp.where(allowed, s, NEG_BIG)
            m_prev = m_ref[br]
            m_new = jnp.maximum(m_prev, jnp.max(s, axis=1, keepdims=True))
            p = jnp.exp(s - m_new)
            alpha = jnp.exp(m_prev - m_new)
            l_ref[br] = alpha * l_ref[br] + jnp.sum(p, axis=1, keepdims=True)
            acc_ref[br] = alpha * acc_ref[br] + _dot(p, vb)
            m_ref[br] = m_new
        return carry

    lax.fori_loop(0, i + 1, step, 0)
    lam = lam_ref[0]
    o = acc_ref[0] / l_ref[0] - lam * (acc_ref[1] / l_ref[1])
    ms = jnp.mean(o * o, axis=-1, keepdims=True)
    o_ref[...] = (o * lax.rsqrt(ms + 1e-5) * sw_ref[...] * out_scale).astype(o_ref.dtype)


def diff_attention(qkv, lam, subln_w, lam_init):
    bsz, s, _ = qkv.shape
    tq, tk = min(ATT_TQ, s), min(ATT_TK, s)
    assert tq == tk and s % tq == 0 and tq % CHUNK == 0
    d = DIFF_HEAD_DIM
    nh = DIFF_HEADS
    kcol = DIFF_WIDTH // d
    vcol = 2 * DIFF_WIDTH // (2 * d)
    grid_spec = pltpu.PrefetchScalarGridSpec(
        num_scalar_prefetch=0,
        grid=(bsz, nh, s // tq),
        in_specs=[
            pl.BlockSpec(memory_space=pltpu.SMEM),
            pl.BlockSpec((None, tq, d), lambda b, h, i: (b, i, 2 * h)),
            pl.BlockSpec((None, tq, d), lambda b, h, i: (b, i, 2 * h + 1)),
            pl.BlockSpec((None, s, d), lambda b, h, i: (b, 0, kcol + 2 * h)),
            pl.BlockSpec((None, s, d), lambda b, h, i: (b, 0, kcol + 2 * h + 1)),
            pl.BlockSpec((None, s, 2 * d), lambda b, h, i: (b, 0, vcol + h)),
            pl.BlockSpec((1, 2 * d), lambda b, h, i: (0, 0)),
        ],
        out_specs=pl.BlockSpec((None, tq, 2 * d), lambda b, h, i: (b, i, h)),
        scratch_shapes=[pltpu.VMEM((2, tq, 1), F32), pltpu.VMEM((2, tq, 1), F32),
                        pltpu.VMEM((2, tq, 2 * d), F32)],
    )
    return pl.pallas_call(
        functools.partial(_diff_attn_kernel, tq=tq, tk=tk, scale=d ** -0.5,
                          out_scale=1.0 - lam_init),
        out_shape=jax.ShapeDtypeStruct((bsz, s, DIFF_WIDTH), BF16),
        grid_spec=grid_spec,
        compiler_params=_cparams(("parallel", "parallel", "arbitrary")),
        name="diff_attn",
    )(lam.reshape(1).astype(F32), qkv, qkv, qkv, qkv, qkv, subln_w.reshape(1, 2 * d))


GLA_TB = 256


def _log_sigmoid(x):
    return jnp.minimum(x, 0.0) - jnp.log(1.0 + jnp.exp(-jnp.abs(x)))


def _gla_kernel(q_ref, k_ref, v_ref, og_ref, cg_ref, gw_ref, gb_ref, nw_ref, o_ref,
                state_ref, *, tb, scale):
    t = pl.program_id(2)

    @pl.when(t == 0)
    def _():
        state_ref[...] = jnp.zeros(state_ref.shape, F32)

    c = CHUNK
    ri = lax.broadcasted_iota(I32, (c, c), 0)
    ci = lax.broadcasted_iota(I32, (c, c), 1)
    causal = ci <= ri
    ltri = causal.astype(F32)
    ones_cols = jnp.ones((c, LANES), F32)
    dv = v_ref.shape[-1]
    for n in range(tb // c):
        rows = slice(n * c, (n + 1) * c)
        gl = _dot32(cg_ref[rows, :], gw_ref[...]) + gb_ref[...]
        log_a = _log_sigmoid(gl) * (1.0 / GLA_GATE_NORM)
        bcum = _dot32(ltri, log_a)
        b_last = bcum[c - 1:c, :]
        q = q_ref[rows, :]
        k = k_ref[rows, :]
        v = v_ref[rows, :]
        q_dec = q * scale * jnp.exp(bcum)
        k_inv = k * jnp.exp(-bcum)
        k_end = k * jnp.exp(b_last - bcum)
        scores = jnp.where(causal, _dot(q_dec, k_inv, _NT), 0.0)
        state = state_ref[...]
        o = _dot(scores, v) + _dot(q_dec, state)
        dec_col = jnp.exp(_dot32(log_a, ones_cols, _TN))
        dec = jnp.concatenate([dec_col] * (dv // LANES), axis=1)
        state_ref[...] = state * dec + _dot(k_end.T, v)
        ms = jnp.mean(o * o, axis=-1, keepdims=True)
        og = og_ref[rows, :]
        gate = og * jax.nn.sigmoid(og)
        o_ref[rows, :] = (o * lax.rsqrt(ms + 1e-5) * nw_ref[...] * gate).astype(o_ref.dtype)


def gla_mixer(proj, cg, gate_w2p, gate_b, norm_w):
    bsz, s, _ = proj.shape
    tb = min(GLA_TB, s)
    assert s % tb == 0 and tb % CHUNK == 0
    dk, dv, nh = GLA_DK, GLA_DV, GLA_HEADS
    in_specs = [
        pl.BlockSpec((None, tb, dk), lambda b, h, t: (b, t, h)),
        pl.BlockSpec((None, tb, dk), lambda b, h, t: (b, t, nh + h)),
        pl.BlockSpec((None, tb, dv), lambda b, h, t: (b, t, nh + h)),
        pl.BlockSpec((None, tb, dv), lambda b, h, t: (b, t, 2 * nh + h)),
        pl.BlockSpec((None, tb, LANES), lambda b, h, t: (b, t, 0)),
        pl.BlockSpec((LANES, dk), lambda b, h, t: (0, h)),
        pl.BlockSpec((1, dk), lambda b, h, t: (0, h)),
        pl.BlockSpec((1, dv), lambda b, h, t: (0, 0)),
    ]
    return pl.pallas_call(
        functools.partial(_gla_kernel, tb=tb, scale=dk ** -0.5),
        out_shape=jax.ShapeDtypeStruct((bsz, s, GLA_WIDTH), BF16),
        grid=(bsz, nh, s // tb),
        in_specs=in_specs,
        out_specs=pl.BlockSpec((None, tb, dv), lambda b, h, t: (b, t, h)),
        scratch_shapes=[pltpu.VMEM((dk, dv), F32)],
        compiler_params=_cparams(("parallel", "parallel", "arbitrary")),
        name="gla",
    )(proj, proj, proj, proj, cg, gate_w2p, gate_b.reshape(1, GLA_QK), norm_w.reshape(1, dv))


def _sb_attn_kernel(q_ref, k_ref, v_ref, u_ref, o_ref, acc_ref, run_ref, *, tq, tk, scale):
    i = pl.program_id(2)
    q = (q_ref[...].astype(F32) * scale).astype(BF16)
    acc_ref[...] = jnp.zeros(acc_ref.shape, F32)
    run_ref[...] = jnp.zeros(run_ref.shape, F32)
    strict = (lax.broadcasted_iota(I32, (tq, tk), 1) < lax.broadcasted_iota(I32, (tq, tk), 0))
    u = u_ref[...]

    def step(jj, carry):
        j = i - jj
        start = pl.multiple_of(j * tk, tk)
        kb = k_ref[pl.ds(start, tk), :]
        vb = v_ref[pl.ds(start, tk), :]
        z = lax.dot_general(q, kb, _NT, preferred_element_type=F32)
        valid = strict | (jj > 0)
        log_keep = jnp.where(valid, jnp.minimum(-z, 0.0) - jnp.log(1.0 + jnp.exp(-jnp.abs(z))), 0.0)
        hi = log_keep.astype(BF16)
        lo = (log_keep - hi.astype(F32)).astype(BF16)
        sums = (lax.dot_general(hi, u, (((1,), (0,)), ((), ())), preferred_element_type=F32)
                + lax.dot_general(lo, u, (((1,), (0,)), ((), ())), preferred_element_type=F32))
        run = run_ref[...]
        between = sums[:, :tk] + jnp.concatenate([run] * (tk // LANES), axis=1)
        w = jnp.where(valid, jnp.exp(z + log_keep + between), 0.0)
        acc_ref[...] += _dot(w, vb)
        run_ref[...] = run + sums[:, tk:]
        return carry

    lax.fori_loop(0, i + 1, step, 0)
    o_ref[...] = acc_ref[...].astype(o_ref.dtype)


def sb_attention(qkv):
    bsz, s, _ = qkv.shape
    tq, tk = min(ATT_TQ, s), min(ATT_TK, s)
    assert tq == tk and s % tq == 0
    d, nh = SB_HEAD_DIM, SB_HEADS
    r = lax.broadcasted_iota(I32, (tk, tk + LANES), 0)
    c = lax.broadcasted_iota(I32, (tk, tk + LANES), 1)
    u = ((r > c) | (c >= tk)).astype(BF16)
    return pl.pallas_call(
        functools.partial(_sb_attn_kernel, tq=tq, tk=tk, scale=d ** -0.5),
        out_shape=jax.ShapeDtypeStruct((bsz, s, SB_WIDTH), BF16),
        grid=(bsz, nh, s // tq),
        in_specs=[
            pl.BlockSpec((None, tq, d), lambda b, h, i: (b, i, h)),
            pl.BlockSpec((None, s, d), lambda b, h, i: (b, 0, nh + h)),
            pl.BlockSpec((None, s, d), lambda b, h, i: (b, 0, 2 * nh + h)),
            pl.BlockSpec((tk, tk + LANES), lambda b, h, i: (0, 0)),
        ],
        out_specs=pl.BlockSpec((None, tq, d), lambda b, h, i: (b, i, h)),
        scratch_shapes=[pltpu.VMEM((tq, d), F32), pltpu.VMEM((tq, LANES), F32)],
        compiler_params=_cparams(("parallel", "parallel", "arbitrary")),
        name="sb_attn",
    )(qkv, qkv, qkv, u)


RWKV_TB = 256
RWKV_PAIR = LANES // RWKV_HEAD_DIM
RWKV_LORA_COLS = 512


def _rwkv_kernel(r_ref, k_ref, v_ref, lo_ref, rp_ref, kp_ref, vp_ref, lop_ref,
                 mur_ref, muk_ref, muv_ref, mulo_ref,
                 w0_ref, a0_ref, kk_ref, ka_ref, rk_ref, gnw_ref, gnb_ref,
                 w2_ref, a2_ref, g2_ref, o_ref, st_ref, *, tb):
    t = pl.program_id(2)
    c = CHUNK
    hd = RWKV_HEAD_DIM

    @pl.when(t == 0)
    def _():
        st_ref[...] = jnp.zeros(st_ref.shape, F32)

    def shifted(x_ref, p_ref, mu_ref):
        x = x_ref[...]
        carry = jnp.where(t == 0, 0.0, p_ref[7:8, :])
        row = lax.broadcasted_iota(I32, x.shape, 0)
        prev = jnp.where(row == 0, carry, pltpu.roll(x, 1, axis=0))
        return x + (prev - x) * mu_ref[...]

    r = shifted(r_ref, rp_ref, mur_ref)
    k = shifted(k_ref, kp_ref, muk_ref)
    v = shifted(v_ref, vp_ref, muv_ref)
    lo = shifted(lo_ref, lop_ref, mulo_ref)
    cw, ca, cg = lo[:, :LANES], lo[:, LANES:2 * LANES], lo[:, 2 * LANES:]

    lw = -RWKV_DECAY_SCALE * jax.nn.sigmoid(w0_ref[...] + _dot(jnp.tanh(cw), w2_ref[...]))
    a = jax.nn.sigmoid(a0_ref[...] + _dot(ca, a2_ref[...]))
    g = _dot(jax.nn.sigmoid(cg), g2_ref[...])

    li = lax.broadcasted_iota(I32, (LANES, LANES), 0)
    lj = lax.broadcasted_iota(I32, (LANES, LANES), 1)
    head_ones = (li // hd == lj // hd).astype(F32)
    eye = li == lj
    strict_blk = (li % c) > (lj % c)
    incl_blk = (li % c) >= (lj % c)

    kk = k * kk_ref[...]
    kk = kk * lax.rsqrt(jnp.maximum(_dot32(kk * kk, head_ones), 1e-24))
    k2 = k * (1.0 + (a - 1.0) * ka_ref[...])
    av = -kk
    bv = kk * a
    bonus = _dot32(r * k2 * rk_ref[...], head_ones) * v

    ci = lax.broadcasted_iota(I32, (c, c), 0)
    cj = lax.broadcasted_iota(I32, (c, c), 1)
    ltri = (cj <= ci).astype(F32)
    lane = lax.broadcasted_iota(I32, (c, LANES), 1)
    head0 = lane < hd

    def stack(x):
        return jnp.concatenate([jnp.where(head0, x, 0.0), jnp.where(head0, 0.0, x)], axis=0)

    ys = []
    for n in range(tb // c):
        rows = slice(n * c, (n + 1) * c)
        lwc = lw[rows]
        cum = _dot32(ltri, lwc)
        tot = cum[c - 1:c, :]
        e_neg = jnp.exp(-cum)
        e_end = jnp.exp(tot - cum)
        r_s = stack(r[rows] * jnp.exp(cum))
        a_s = stack(av[rows] * jnp.exp(cum - lwc))
        b_s = stack(bv[rows] * e_neg)
        k_s = stack(k2[rows] * e_neg)
        bh_s = stack(bv[rows] * e_end)
        kh_s = stack(k2[rows] * e_end)
        v_s = stack(v[rows])

        big = _dot(jnp.concatenate([a_s, r_s], axis=0), jnp.concatenate([b_s, k_s], axis=0), _NT)
        nn = jnp.where(strict_blk, big[:LANES, :LANES], 0.0)
        ak = jnp.where(strict_blk, big[:LANES, LANES:], 0.0)
        rb = jnp.where(incl_blk, big[LANES:, :LANES], 0.0)
        rk = jnp.where(incl_blk, big[LANES:, LANES:], 0.0)

        tinv = eye.astype(F32) + nn
        npow = nn
        for _ in range(5):
            npow = _dot(npow, npow)
            tinv = tinv + _dot(tinv, npow)

        p1 = _dot(tinv, a_s)
        p2 = _dot(tinv, _dot(ak, v_s))
        pv = jnp.concatenate([p2, v_s], axis=0)
        q1 = r_s + _dot(rb, p1)
        q2 = _dot(jnp.concatenate([rb, rk], axis=1), pv)
        m1 = jnp.where(eye, jnp.exp(tot), 0.0) + _dot(bh_s.T, p1)
        m2 = _dot(jnp.concatenate([bh_s, kh_s], axis=0).T, pv)

        st = st_ref[...]
        y_s = _dot(q1, st) + q2
        st_ref[...] = _dot(m1, st) + m2
        ys.append(y_s[:c] + y_s[c:])

    y = jnp.concatenate(ys, axis=0)
    mu = _dot32(y, head_ones) * (1.0 / hd)
    yc = y - mu
    var = _dot32(yc * yc, head_ones) * (1.0 / hd)
    yn = yc * lax.rsqrt(var + RWKV_GN_EPS) * gnw_ref[...] + gnb_ref[...]
    o_ref[...] = ((yn + bonus) * g).astype(o_ref.dtype)


def rwkv7_mixer(feat, mu, w0, a0, k_k, k_a, r_k, gn_w, gn_b, w2p, a2p, g2):
    bsz, s, _ = feat.shape
    tb = min(RWKV_TB, s)
    assert s % tb == 0 and tb % CHUNK == 0
    w = RWKV_WIDTH
    npair = w // LANES
    nlo = RWKV_LORA_COLS
    lo_blk = 3 * w // nlo
    pr = tb // 8

    def cur(width, col):
        return pl.BlockSpec((None, tb, width), lambda b, p, t: (b, t, col(p)))

    def prv(width, col):
        return pl.BlockSpec((None, 8, width),
                            lambda b, p, t: (b, jnp.maximum(t * pr - 1, 0), col(p)))

    def par(arr_cols, col):
        return pl.BlockSpec((1, arr_cols), lambda b, p, t: (0, col(p)))

    cr, ck, cv = (lambda p: p), (lambda p: npair + p), (lambda p: 2 * npair + p)
    clo = lambda p: lo_blk
    vec = lambda x: x.reshape(1, -1)
    in_specs = [
        cur(LANES, cr), cur(LANES, ck), cur(LANES, cv), cur(nlo, clo),
        prv(LANES, cr), prv(LANES, ck), prv(LANES, cv), prv(nlo, clo),
        par(LANES, cr), par(LANES, ck), par(LANES, cv), par(nlo, clo),
    ] + [par(LANES, cr)] * 7 + [
        pl.BlockSpec((LANES, LANES), lambda b, p, t: (0, p)),
        pl.BlockSpec((LANES, LANES), lambda b, p, t: (0, p)),
        pl.BlockSpec((LORA_G, LANES), lambda b, p, t: (0, p)),
    ]
    return pl.pallas_call(
        functools.partial(_rwkv_kernel, tb=tb),
        out_shape=jax.ShapeDtypeStruct((bsz, s, w), BF16),
        grid=(bsz, npair, s // tb),
        in_specs=in_specs,
        out_specs=pl.BlockSpec((None, tb, LANES), lambda b, p, t: (b, t, p)),
        scratch_shapes=[pltpu.VMEM((LANES, LANES), F32)],
        compiler_params=_cparams(("parallel", "parallel", "arbitrary")),
        name="rwkv7",
    )(feat, feat, feat, feat, feat, feat, feat, feat,
      vec(mu), vec(mu), vec(mu), vec(mu),
      vec(w0), vec(a0), vec(k_k), vec(k_a), vec(r_k), vec(gn_w), vec(gn_b),
      w2p, a2p, g2)


def _pad_cols(x, n):
    return jnp.pad(x, ((0, 0), (0, n - x.shape[1])))


def _pad_rows(x, n):
    return jnp.pad(x, ((0, n - x.shape[0]), (0, 0)))


def _rwkv_cols(x):
    w = RWKV_WIDTH
    main = x[..., :3 * w]
    cw = x[..., 3 * w:3 * w + LORA_W]
    ca = x[..., 3 * w + LORA_W:3 * w + LORA_W + LORA_A]
    cg = x[..., 3 * w + LORA_W + LORA_A:]
    pad = lambda y: jnp.pad(y, [(0, 0)] * (y.ndim - 1) + [(0, LANES - y.shape[-1])])
    return jnp.concatenate([main, pad(cw), pad(ca), cg], axis=-1)


PROJ_TM = 1024
PROJ_TN = 512
ROUTER_COLS = LANES


def _router_params(rg, rgb, re, reb):
    d = rg.shape[0]
    w = jnp.zeros((d, ROUTER_COLS), F32).at[:, :N_GROUPS].set(rg)
    w = w.at[:, N_GROUPS:N_GROUPS + N_EXPERTS].set(re)
    b = jnp.zeros((ROUTER_COLS,), F32).at[:N_GROUPS].set(rgb)
    b = b.at[N_GROUPS:N_GROUPS + N_EXPERTS].set(reb)
    return w, b


def _tail(h_f32, mix_bf16, w_out, ln1_g, ln1_b, rg, rgb, re, reb, wg, wu, wd, ln2_g, ln2_b):
    mixed = matmul(mix_bf16, w_out.astype(BF16), out_dtype=F32, tm=PROJ_TM, tn=PROJ_TN)
    rw, rb = _router_params(rg, rgb, re, reb)
    h1_f32, _, logits = ln_router(h_f32, mixed, ln1_g, ln1_b, rw, rb)
    return hier_moe_ln(h1_f32, logits, wg.astype(BF16), wu.astype(BF16), wd.astype(BF16),
                       ln2_g, ln2_b)


def kernel(x, l0_w_in, l0_shift_mu, l0_w0, l0_w2, l0_a0, l0_a2, l0_g2, l0_k_k, l0_k_a, l0_r_k, l0_gn_w, l0_gn_b, l0_w_out, l0_ln1_g, l0_ln1_b, l0_router_g, l0_router_g_b, l0_router_e, l0_router_e_b, l0_w_gate, l0_w_up, l0_w_down, l0_ln2_g, l0_ln2_b, l1_w_in, l1_lq1, l1_lk1, l1_lq2, l1_lk2, l1_subln_w, l1_gate_w2, l1_gate_b, l1_gla_norm_w, l1_w_out, l1_ln1_g, l1_ln1_b, l1_router_g, l1_router_g_b, l1_router_e, l1_router_e_b, l1_w_gate, l1_w_up, l1_w_down, l1_ln2_g, l1_ln2_b):
    bsz, s, d = x.shape
    t = bsz * s
    h0 = x.reshape(t, d)

    sbw = 3 * SB_WIDTH
    x_bf16 = h0.astype(BF16)
    qkv0 = matmul(x_bf16, l0_w_in[:, :sbw].astype(BF16), out_dtype=BF16, tm=PROJ_TM, tn=PROJ_TN)
    feat = matmul(x_bf16, _rwkv_cols(l0_w_in[:, sbw:]).astype(BF16), out_dtype=F32,
                  tm=PROJ_TM, tn=PROJ_TN)
    o_sb = sb_attention(qkv0.reshape(bsz, s, sbw))
    o_rw = rwkv7_mixer(feat.reshape(bsz, s, -1), _rwkv_cols(l0_shift_mu[None])[0],
                       l0_w0, l0_a0, l0_k_k, l0_k_a, l0_r_k.reshape(-1), l0_gn_w, l0_gn_b,
                       _pad_rows(l0_w2, LANES), _pad_rows(l0_a2, LANES), l0_g2)
    mix0 = jnp.concatenate([o_sb, o_rw], axis=-1).reshape(t, d)
    h2_f32, h2_bf16 = _tail(h0, mix0, l0_w_out, l0_ln1_g, l0_ln1_b, l0_router_g, l0_router_g_b,
                            l0_router_e, l0_router_e_b, l0_w_gate, l0_w_up, l0_w_down,
                            l0_ln2_g, l0_ln2_b)

    dfw = 3 * DIFF_WIDTH
    g0 = dfw + 2 * GLA_QK + GLA_WIDTH
    g1 = g0 + GLA_GATE_RANK
    qkv1 = matmul(h2_bf16, l1_w_in[:, :dfw].astype(BF16), out_dtype=BF16, tm=PROJ_TM, tn=PROJ_TN)
    w_gla = jnp.concatenate([l1_w_in[:, dfw:g0], l1_w_in[:, g1:]], axis=1).astype(BF16)
    gproj = matmul(h2_bf16, w_gla, out_dtype=F32, tm=PROJ_TM, tn=PROJ_TN)
    cgp = matmul(h2_f32, _pad_cols(l1_w_in[:, g0:g1], LANES), out_dtype=F32, tm=512, tn=LANES,
                 exact=True)
    lam_init = 0.8 - 0.6 * math.exp(-0.3 * 1)
    lam = (jnp.exp(jnp.sum(l1_lq1 * l1_lk1)) - jnp.exp(jnp.sum(l1_lq2 * l1_lk2)) + lam_init)
    o_diff = diff_attention(qkv1.reshape(bsz, s, dfw), lam, l1_subln_w, lam_init)
    o_gla = gla_mixer(gproj.reshape(bsz, s, -1), cgp.reshape(bsz, s, LANES),
                      _pad_rows(l1_gate_w2, LANES), l1_gate_b, l1_gla_norm_w)
    mix1 = jnp.concatenate([o_diff, o_gla], axis=-1).reshape(t, d)
    out_f32, _ = _tail(h2_f32, mix1, l1_w_out, l1_ln1_g, l1_ln1_b, l1_router_g, l1_router_g_b,
                       l1_router_e, l1_router_e_b, l1_w_gate, l1_w_up, l1_w_down,
                       l1_ln2_g, l1_ln2_b)
    return out_f32.reshape(bsz, s, d)
```

```python
import functools
import math

import jax
import jax.numpy as jnp
from jax import lax
from jax.experimental import pallas as pl
from jax.experimental.pallas import tpu as pltpu

F32 = jnp.float32
BF16 = jnp.bfloat16
I32 = jnp.int32

D_MODEL = 4096
DEPTH = 2
CHUNK = 64
SB_WIDTH = 2048
SB_HEAD_DIM = 128
SB_HEADS = SB_WIDTH // SB_HEAD_DIM
RWKV_WIDTH = 2048
RWKV_HEAD_DIM = 64
LORA_W = 96
LORA_A = 96
LORA_G = 256
RWKV_DECAY_SCALE = math.exp(-0.5)
RWKV_GN_EPS = 64e-5
DIFF_WIDTH = 2048
DIFF_HEAD_DIM = 128
DIFF_HEADS = DIFF_WIDTH // (2 * DIFF_HEAD_DIM)
GLA_WIDTH = 2048
GLA_HEADS = 4
GLA_QK = GLA_WIDTH // 2
GLA_DK = GLA_QK // GLA_HEADS
GLA_DV = GLA_WIDTH // GLA_HEADS
GLA_GATE_RANK = 16
GLA_GATE_NORM = 16.0
N_GROUPS = 4
EXPERTS_PER_GROUP = 8
N_EXPERTS = N_GROUPS * EXPERTS_PER_GROUP
TOP_K = 2
D_EXPERT = 768
DEEPNORM_ALPHA = (2 * DEPTH) ** 0.25
LN_EPS = 1e-5

LANES = 128
V7X_VMEM_LIMIT = 56 * 1024 * 1024

HIGHEST = lax.Precision.HIGHEST


def _cparams(semantics, vmem=V7X_VMEM_LIMIT):
    return pltpu.CompilerParams(dimension_semantics=semantics, vmem_limit_bytes=vmem)


def _dot(a, b, dims=(((1,), (0,)), ((), ()))):
    return lax.dot_general(a.astype(BF16), b.astype(BF16), dims, preferred_element_type=F32)


def _dot32(a, b, dims=(((1,), (0,)), ((), ()))):
    return lax.dot_general(a.astype(F32), b.astype(F32), dims, precision=HIGHEST,
                           preferred_element_type=F32)


_NT = (((1,), (1,)), ((), ()))
_TN = (((0,), (0,)), ((), ()))


def _mm_kernel(x_ref, w_ref, o_ref, *, exact):
    if exact:
        acc = _dot32(x_ref[...], w_ref[...])
    else:
        acc = _dot(x_ref[...], w_ref[...])
    o_ref[...] = acc.astype(o_ref.dtype)


def matmul(x, w, *, out_dtype, tm, tn, exact=False):
    m, k = x.shape
    n = w.shape[1]
    assert m % tm == 0 and n % tn == 0, (x.shape, w.shape, tm, tn)
    return pl.pallas_call(
        functools.partial(_mm_kernel, exact=exact),
        out_shape=jax.ShapeDtypeStruct((m, n), out_dtype),
        grid=(m // tm, n // tn),
        in_specs=[pl.BlockSpec((tm, k), lambda i, j: (i, 0)),
                  pl.BlockSpec((k, tn), lambda i, j: (0, j))],
        out_specs=pl.BlockSpec((tm, tn), lambda i, j: (i, j)),
        compiler_params=_cparams(("parallel", "arbitrary")),
        name="proj",
    )(x, w)


def _layer_norm_rows(x, g, b):
    mu = jnp.mean(x, axis=-1, keepdims=True)
    xc = x - mu
    var = jnp.mean(xc * xc, axis=-1, keepdims=True)
    return xc * lax.rsqrt(var + LN_EPS) * g + b


def _ln_router_kernel(h_ref, m_ref, g_ref, b_ref, r_ref, rb_ref, of_ref, ob_ref, lg_ref):
    y = _layer_norm_rows(DEEPNORM_ALPHA * h_ref[...] + m_ref[...], g_ref[...], b_ref[...])
    of_ref[...] = y
    ob_ref[...] = y.astype(BF16)
    lg_ref[...] = _dot32(y, r_ref[...]) + rb_ref[...]


def ln_router(h, mixed, g, b, router_w, router_b, *, tm=256):
    t, d = h.shape
    nr = router_w.shape[1]
    row = lambda i: (i, 0)
    fixed = lambda i: (0, 0)
    return pl.pallas_call(
        _ln_router_kernel,
        out_shape=(jax.ShapeDtypeStruct((t, d), F32), jax.ShapeDtypeStruct((t, d), BF16),
                   jax.ShapeDtypeStruct((t, nr), F32)),
        grid=(t // tm,),
        in_specs=[pl.BlockSpec((tm, d), row), pl.BlockSpec((tm, d), row),
                  pl.BlockSpec((1, d), fixed), pl.BlockSpec((1, d), fixed),
                  pl.BlockSpec((d, nr), fixed), pl.BlockSpec((1, nr), fixed)],
        out_specs=(pl.BlockSpec((tm, d), row), pl.BlockSpec((tm, d), row),
                   pl.BlockSpec((tm, nr), row)),
        compiler_params=_cparams(("parallel",)),
        name="ln_router",
    )(h, mixed, g.reshape(1, d), b.reshape(1, d), router_w, router_b.reshape(1, nr))


MOE_ROWS = 512
MOE_ESPLIT = 3


def _moe_ffn_kernel(tok_ref, be_ref, nb_ref, h_hbm, wgu_ref, wd_ref, o_ref,
                    xrows, xb16, sem, *, rows):
    i = pl.program_id(0)
    j = pl.program_id(1)

    def row_copy(r, t):
        return pltpu.make_async_copy(h_hbm.at[pl.ds(t, 1), :], xrows.at[pl.ds(r, 1), :], sem)

    @pl.when(i < nb_ref[0])
    def _():
        @pl.when(j == 0)
        def _():
            def issue(r, c):
                row_copy(r, tok_ref[i * rows + r]).start()
                return c
            lax.fori_loop(0, rows, issue, 0)

            def drain(r, c):
                row_copy(r, 0).wait()
                return c
            lax.fori_loop(0, rows, drain, 0)
            xb16[...] = xrows[...].astype(BF16)

        hgu = _dot(xb16[...], wgu_ref[...])
        ce = hgu.shape[1] // 2
        hg, hu = hgu[:, :ce], hgu[:, ce:]
        act = hg * jax.nn.sigmoid(hg) * hu
        part = _dot(act, wd_ref[...])

        @pl.when(j == 0)
        def _():
            o_ref[...] = part

        @pl.when(j > 0)
        def _():
            o_ref[...] += part

    @pl.when((i >= nb_ref[0]) & (j == 0))
    def _():
        o_ref[...] = jnp.zeros_like(o_ref)


def _moe_weight_slabs(wg, wu, wd):
    e, d, de = wg.shape
    ce = de // MOE_ESPLIT
    gu = jnp.concatenate([wg.reshape(e, d, MOE_ESPLIT, ce), wu.reshape(e, d, MOE_ESPLIT, ce)],
                         axis=-1)
    return gu.transpose(0, 2, 1, 3).astype(BF16), wd.astype(BF16)


def moe_ffn(h, slot_tok, block_e, nb_used, wgu, wd):
    t, d = h.shape
    rows = MOE_ROWS
    nblk = slot_tok.shape[0] // rows
    ce = wd.shape[1] // MOE_ESPLIT

    def jj(i, j, nb):
        return jnp.where(i < nb[0], j, MOE_ESPLIT - 1)

    grid_spec = pltpu.PrefetchScalarGridSpec(
        num_scalar_prefetch=3,
        grid=(nblk, MOE_ESPLIT),
        in_specs=[
            pl.BlockSpec(memory_space=pl.ANY),
            pl.BlockSpec((None, None, d, 2 * ce),
                         lambda i, j, tok, be, nb: (be[i], jj(i, j, nb), 0, 0)),
            pl.BlockSpec((None, ce, d), lambda i, j, tok, be, nb: (be[i], jj(i, j, nb), 0)),
        ],
        out_specs=pl.BlockSpec((rows, d), lambda i, j, tok, be, nb: (i, 0)),
        scratch_shapes=[pltpu.VMEM((rows, d), F32), pltpu.VMEM((rows, d), BF16),
                        pltpu.SemaphoreType.DMA(())],
    )
    return pl.pallas_call(
        functools.partial(_moe_ffn_kernel, rows=rows),
        out_shape=jax.ShapeDtypeStruct((nblk * rows, d), F32),
        grid_spec=grid_spec,
        compiler_params=_cparams(("arbitrary", "arbitrary")),
        name="moe_ffn",
    )(slot_tok, block_e, nb_used, h, wgu, wd)


def _combine_ln_kernel(pos_ref, h_ref, gate_ref, yb_hbm, g_ref, b_ref, of_ref, ob_ref,
                       ybuf, sem, *, tm):
    i = pl.program_id(0)

    def row_copy(k, r, p):
        return pltpu.make_async_copy(yb_hbm.at[pl.ds(p, 1), :], ybuf.at[k, pl.ds(r, 1), :], sem)

    def issue(r, c):
        a = (i * tm + r) * TOP_K
        for k in range(TOP_K):
            row_copy(k, r, pos_ref[a + k]).start()
        return c
    lax.fori_loop(0, tm, issue, 0)

    def drain(r, c):
        for k in range(TOP_K):
            row_copy(k, r, 0).wait()
        return c
    lax.fori_loop(0, tm, drain, 0)

    gate = gate_ref[...]
    y = ybuf[0] * gate[:, 0:1]
    for k in range(1, TOP_K):
        y = y + ybuf[k] * gate[:, k:k + 1]
    out = _layer_norm_rows(DEEPNORM_ALPHA * h_ref[...] + y, g_ref[...], b_ref[...])
    of_ref[...] = out
    ob_ref[...] = out.astype(BF16)


def combine_ln(h, gates, pos, yb, g, b, *, tm=256):
    t, d = h.shape
    grid_spec = pltpu.PrefetchScalarGridSpec(
        num_scalar_prefetch=1,
        grid=(t // tm,),
        in_specs=[
            pl.BlockSpec((tm, d), lambda i, pos: (i, 0)),
            pl.BlockSpec((tm, TOP_K), lambda i, pos: (i, 0)),
            pl.BlockSpec(memory_space=pl.ANY),
            pl.BlockSpec((1, d), lambda i, pos: (0, 0)),
            pl.BlockSpec((1, d), lambda i, pos: (0, 0)),
        ],
        out_specs=(pl.BlockSpec((tm, d), lambda i, pos: (i, 0)),
                   pl.BlockSpec((tm, d), lambda i, pos: (i, 0))),
        scratch_shapes=[pltpu.VMEM((TOP_K, tm, d), F32), pltpu.SemaphoreType.DMA(())],
    )
    return pl.pallas_call(
        functools.partial(_combine_ln_kernel, tm=tm),
        out_shape=(jax.ShapeDtypeStruct((t, d), F32), jax.ShapeDtypeStruct((t, d), BF16)),
        grid_spec=grid_spec,
        compiler_params=_cparams(("arbitrary",)),
        name="combine_ln",
    )(pos.reshape(-1), h, gates, yb, g.reshape(1, d), b.reshape(1, d))


def _routing_tables(logits, t):
    g_logits = logits[:, :N_GROUPS]
    g_prob = jax.nn.softmax(g_logits, axis=-1)
    g_sel = jnp.argmax(g_logits, axis=-1)
    g_w = jnp.take_along_axis(g_prob, g_sel[:, None], axis=-1)[:, 0]
    e_logits = logits[:, N_GROUPS:N_GROUPS + N_EXPERTS].reshape(t, N_GROUPS, EXPERTS_PER_GROUP)
    e_in_group = jnp.take_along_axis(e_logits, g_sel[:, None, None], axis=1)[:, 0]
    top_vals, top_idx = lax.top_k(e_in_group, TOP_K)
    gates = jax.nn.softmax(top_vals, axis=-1) * g_w[:, None]
    expert_id = (g_sel[:, None] * EXPERTS_PER_GROUP + top_idx).astype(I32)

    a = t * TOP_K
    rows = MOE_ROWS
    flat_e = expert_id.reshape(a)
    flat_tok = jnp.repeat(jnp.arange(t, dtype=I32), TOP_K)
    order = jnp.argsort(flat_e)
    e_sorted = flat_e[order]
    counts = jnp.zeros((N_EXPERTS,), I32).at[flat_e].add(1)
    padded = (counts + rows - 1) // rows * rows
    pad_end = jnp.cumsum(padded)
    pad_start = pad_end - padded
    start = jnp.cumsum(counts) - counts
    dest = pad_start[e_sorted] + (jnp.arange(a, dtype=I32) - start[e_sorted])
    nblk = a // rows + N_EXPERTS
    slot_tok = jnp.zeros((nblk * rows,), I32).at[dest].set(flat_tok[order])
    pos = jnp.zeros((a,), I32).at[order].set(dest.astype(I32))
    block_e = jnp.minimum(
        jnp.searchsorted(pad_end, jnp.arange(nblk, dtype=I32) * rows, side='right'),
        N_EXPERTS - 1).astype(I32)
    nb_used = (pad_end[-1] // rows).astype(I32).reshape(1)
    return gates.astype(F32), pos.reshape(t, TOP_K), slot_tok, block_e, nb_used


def hier_moe_ln(h_f32, logits, wg, wu, wd, ln_g, ln_b):
    t = h_f32.shape[0]
    gates, pos, slot_tok, block_e, nb_used = _routing_tables(logits, t)
    wgu, wd16 = _moe_weight_slabs(wg, wu, wd)
    yb = moe_ffn(h_f32, slot_tok, block_e, nb_used, wgu, wd16)
    return combine_ln(h_f32, gates, pos, yb, ln_g, ln_b)


DIFF_T = 512
NEG_BIG = -1e30


def _lane_tile(x, n):
    return x if n == 1 else jnp.concatenate([x] * n, axis=1)


def _diff_attn_kernel(lam_ref, q1_ref, q2_ref, k1_ref, k2_ref, v_ref, sw_ref, o_ref,
                      m_ref, l_ref, acc_ref, *, t, scale, out_scale):
    i = pl.program_id(2)
    br2 = range(2)
    qs = [(q1_ref[...].astype(F32) * scale).astype(BF16),
          (q2_ref[...].astype(F32) * scale).astype(BF16)]
    k_refs = [k1_ref, k2_ref]
    m_ref[...] = jnp.full(m_ref.shape, NEG_BIG, F32)
    l_ref[...] = jnp.zeros(l_ref.shape, F32)
    acc_ref[...] = jnp.zeros(acc_ref.shape, F32)
    krep = t // LANES
    vrep = v_ref.shape[-1] // LANES

    def block(j, diagonal):
        start = pl.multiple_of(j * t, t)
        vb = v_ref[pl.ds(start, t), :]
        s = [lax.dot_general(qs[b], k_refs[b][pl.ds(start, t), :], _NT,
                             preferred_element_type=F32) for b in br2]
        if diagonal:
            row_chunk = lax.broadcasted_iota(I32, (t, t), 0) // CHUNK
            col_chunk = lax.broadcasted_iota(I32, (t, t), 1) // CHUNK
            s = [jnp.where(col_chunk <= row_chunk, s[b], NEG_BIG) for b in br2]
        m_prev = [m_ref[b] for b in br2]
        m_new = [jnp.maximum(m_prev[b], jnp.max(s[b], axis=1, keepdims=True)) for b in br2]
        p = [jnp.exp(s[b] - _lane_tile(m_new[b], krep)) for b in br2]
        alpha = [jnp.exp(m_prev[b] - m_new[b]) for b in br2]
        l_new = [alpha[b] * l_ref[b] + jnp.sum(p[b], axis=1, keepdims=True) for b in br2]
        pv = [_dot(p[b], vb) for b in br2]
        for b in br2:
            acc_ref[b] = _lane_tile(alpha[b], vrep) * acc_ref[b] + pv[b]
            m_ref[b] = m_new[b]
            l_ref[b] = l_new[b]

    def full_block(j, carry):
        block(j, False)
        return carry

    lax.fori_loop(0, i, full_block, 0)
    block(i, True)
    lam = lam_ref[0]
    inv = [_lane_tile(1.0 / l_ref[b], vrep) for b in br2]
    o = acc_ref[0] * inv[0] - lam * (acc_ref[1] * inv[1])
    ms = jnp.mean(o * o, axis=-1, keepdims=True)
    o_ref[...] = (o * lax.rsqrt(ms + 1e-5) * sw_ref[...] * out_scale).astype(o_ref.dtype)


def diff_attention(qkv, lam, subln_w, lam_init):
    bsz, s, _ = qkv.shape
    tq = min(DIFF_T, s)
    assert s % tq == 0 and tq % CHUNK == 0
    d = DIFF_HEAD_DIM
    nh = DIFF_HEADS
    kcol = DIFF_WIDTH // d
    vcol = 2 * DIFF_WIDTH // (2 * d)
    grid_spec = pltpu.PrefetchScalarGridSpec(
        num_scalar_prefetch=0,
        grid=(bsz, nh, s // tq),
        in_specs=[
            pl.BlockSpec(memory_space=pltpu.SMEM),
            pl.BlockSpec((None, tq, d), lambda b, h, i: (b, i, 2 * h)),
            pl.BlockSpec((None, tq, d), lambda b, h, i: (b, i, 2 * h + 1)),
            pl.BlockSpec((None, s, d), lambda b, h, i: (b, 0, kcol + 2 * h)),
            pl.BlockSpec((None, s, d), lambda b, h, i: (b, 0, kcol + 2 * h + 1)),
            pl.BlockSpec((None, s, 2 * d), lambda b, h, i: (b, 0, vcol + h)),
            pl.BlockSpec((1, 2 * d), lambda b, h, i: (0, 0)),
        ],
        out_specs=pl.BlockSpec((None, tq, 2 * d), lambda b, h, i: (b, i, h)),
        scratch_shapes=[pltpu.VMEM((2, tq, LANES), F32), pltpu.VMEM((2, tq, LANES), F32),
                        pltpu.VMEM((2, tq, 2 * d), F32)],
    )
    return pl.pallas_call(
        functools.partial(_diff_attn_kernel, t=tq, scale=d ** -0.5,
                          out_scale=1.0 - lam_init),
        out_shape=jax.ShapeDtypeStruct((bsz, s, DIFF_WIDTH), BF16),
        grid_spec=grid_spec,
        compiler_params=_cparams(("parallel", "parallel", "arbitrary")),
        name="diff_attn",
    )(lam.reshape(1).astype(F32), qkv, qkv, qkv, qkv, qkv, subln_w.reshape(1, 2 * d))


GLA_TB = 256


def _log_sigmoid(x):
    return jnp.minimum(x, 0.0) - jnp.log(1.0 + jnp.exp(-jnp.abs(x)))


def _gla_kernel(q_ref, k_ref, v_ref, og_ref, cg_ref, gw_ref, gb_ref, nw_ref, o_ref,
                state_ref, *, tb, scale):
    t = pl.program_id(2)

    @pl.when(t == 0)
    def _():
        state_ref[...] = jnp.zeros(state_ref.shape, F32)

    c = CHUNK
    ri = lax.broadcasted_iota(I32, (c, c), 0)
    ci = lax.broadcasted_iota(I32, (c, c), 1)
    causal = ci <= ri
    ltri = causal.astype(F32)
    ones_cols = jnp.ones((c, LANES), F32)
    dv = v_ref.shape[-1]
    for n in range(tb // c):
        rows = slice(n * c, (n + 1) * c)
        gl = _dot32(cg_ref[rows, :], gw_ref[...]) + gb_ref[...]
        log_a = _log_sigmoid(gl) * (1.0 / GLA_GATE_NORM)
        bcum = _dot32(ltri, log_a)
        b_last = bcum[c - 1:c, :]
        q = q_ref[rows, :]
        k = k_ref[rows, :]
        v = v_ref[rows, :]
        q_dec = q * scale * jnp.exp(bcum)
        k_inv = k * jnp.exp(-bcum)
        k_end = k * jnp.exp(b_last - bcum)
        scores = jnp.where(causal, _dot(q_dec, k_inv, _NT), 0.0)
        state = state_ref[...]
        o = _dot(scores, v) + _dot(q_dec, state)
        dec_col = jnp.exp(_dot32(log_a, ones_cols, _TN))
        dec = jnp.concatenate([dec_col] * (dv // LANES), axis=1)
        state_ref[...] = state * dec + _dot(k_end.T, v)
        ms = jnp.mean(o * o, axis=-1, keepdims=True)
        og = og_ref[rows, :]
        gate = og * jax.nn.sigmoid(og)
        o_ref[rows, :] = (o * lax.rsqrt(ms + 1e-5) * nw_ref[...] * gate).astype(o_ref.dtype)


def gla_mixer(proj, cg, gate_w2p, gate_b, norm_w):
    bsz, s, _ = proj.shape
    tb = min(GLA_TB, s)
    assert s % tb == 0 and tb % CHUNK == 0
    dk, dv, nh = GLA_DK, GLA_DV, GLA_HEADS
    in_specs = [
        pl.BlockSpec((None, tb, dk), lambda b, h, t: (b, t, h)),
        pl.BlockSpec((None, tb, dk), lambda b, h, t: (b, t, nh + h)),
        pl.BlockSpec((None, tb, dv), lambda b, h, t: (b, t, nh + h)),
        pl.BlockSpec((None, tb, dv), lambda b, h, t: (b, t, 2 * nh + h)),
        pl.BlockSpec((None, tb, LANES), lambda b, h, t: (b, t, 0)),
        pl.BlockSpec((LANES, dk), lambda b, h, t: (0, h)),
        pl.BlockSpec((1, dk), lambda b, h, t: (0, h)),
        pl.BlockSpec((1, dv), lambda b, h, t: (0, 0)),
    ]
    return pl.pallas_call(
        functools.partial(_gla_kernel, tb=tb, scale=dk ** -0.5),
        out_shape=jax.ShapeDtypeStruct((bsz, s, GLA_WIDTH), BF16),
        grid=(bsz, nh, s // tb),
        in_specs=in_specs,
        out_specs=pl.BlockSpec((None, tb, dv), lambda b, h, t: (b, t, h)),
        scratch_shapes=[pltpu.VMEM((dk, dv), F32)],
        compiler_params=_cparams(("parallel", "parallel", "arbitrary")),
        name="gla",
    )(proj, proj, proj, proj, cg, gate_w2p, gate_b.reshape(1, GLA_QK), norm_w.reshape(1, dv))


SB_T = 256
SB_HEADS_PER_STEP = 2
SB_UNDERFLOW = 110.0


def _sb_attn_kernel(q_ref, k_ref, v_ref, u_ref, o_ref, acc_ref, run_ref, *, t, nhs, scale):
    i = pl.program_id(2)
    d = SB_HEAD_DIM
    heads = range(nhs)
    cols = [slice(h * d, (h + 1) * d) for h in heads]
    q = [(q_ref[:, cols[h]].astype(F32) * scale).astype(BF16) for h in heads]
    u = u_ref[...]
    rep = t // LANES

    def block(j, diagonal, first):
        start = pl.multiple_of(j * t, t)
        kb = [k_ref[pl.ds(start, t), cols[h]] for h in heads]
        vb = [v_ref[pl.ds(start, t), cols[h]] for h in heads]
        z = [lax.dot_general(q[h], kb[h], _NT, preferred_element_type=F32) for h in heads]
        sp = [jnp.maximum(z[h], 0.0) + jnp.log(1.0 + jnp.exp(-jnp.abs(z[h]))) for h in heads]
        if diagonal:
            strict = (lax.broadcasted_iota(I32, (t, t), 1) < lax.broadcasted_iota(I32, (t, t), 0))
            sp = [jnp.where(strict, sp[h], 0.0) for h in heads]
        hi = [sp[h].astype(BF16) for h in heads]
        lo = [(sp[h] - hi[h].astype(F32)).astype(BF16) for h in heads]
        sums = [lax.dot_general(hi[h], u, (((1,), (0,)), ((), ())), preferred_element_type=F32)
                + lax.dot_general(lo[h], u, (((1,), (0,)), ((), ())), preferred_element_type=F32)
                for h in heads]
        if first:
            x = [z[h] + sums[h] for h in heads]
        else:
            x = [z[h] + sums[h] + _lane_tile(run_ref[h], rep) for h in heads]
        w = [jnp.exp(x[h]) for h in heads]
        if diagonal:
            w = [jnp.where(strict, w[h], 0.0) for h in heads]
        pv = [_dot(w[h], vb[h]) for h in heads]
        total = [jnp.broadcast_to(sums[h][:, 0:1], (t, LANES)) for h in heads]
        run_max = None
        for h in heads:
            if first:
                acc_ref[:, cols[h]] = pv[h]
                run_new = total[h]
            else:
                acc_ref[:, cols[h]] += pv[h]
                run_new = run_ref[h] + total[h]
            run_ref[h] = run_new
            m = jnp.max(run_new)
            run_max = m if run_max is None else jnp.maximum(run_max, m)
        return (run_max > -SB_UNDERFLOW).astype(I32)

    alive0 = block(i, True, True)

    def cond(c):
        return (c[0] <= i) & (c[1] > 0)

    def body(c):
        return c[0] + 1, block(i - c[0], False, False)

    lax.while_loop(cond, body, (jnp.int32(1), alive0))
    o_ref[...] = acc_ref[...].astype(o_ref.dtype)


def sb_attention(qkv):
    bsz, s, _ = qkv.shape
    t = min(SB_T, s)
    assert s % t == 0
    nhs = SB_HEADS_PER_STEP
    w = nhs * SB_HEAD_DIM
    ng = SB_HEADS // nhs
    r = lax.broadcasted_iota(I32, (t, t), 0)
    c = lax.broadcasted_iota(I32, (t, t), 1)
    u = jnp.where(r >= c, -1.0, 0.0).astype(BF16)
    return pl.pallas_call(
        functools.partial(_sb_attn_kernel, t=t, nhs=nhs, scale=SB_HEAD_DIM ** -0.5),
        out_shape=jax.ShapeDtypeStruct((bsz, s, SB_WIDTH), BF16),
        grid=(bsz, ng, s // t),
        in_specs=[
            pl.BlockSpec((None, t, w), lambda b, g, i: (b, i, g)),
            pl.BlockSpec((None, s, w), lambda b, g, i: (b, 0, ng + g)),
            pl.BlockSpec((None, s, w), lambda b, g, i: (b, 0, 2 * ng + g)),
            pl.BlockSpec((t, t), lambda b, g, i: (0, 0)),
        ],
        out_specs=pl.BlockSpec((None, t, w), lambda b, g, i: (b, i, g)),
        scratch_shapes=[pltpu.VMEM((t, w), F32), pltpu.VMEM((nhs, t, LANES), F32)],
        compiler_params=_cparams(("parallel", "parallel", "arbitrary")),
        name="sb_attn",
    )(qkv, qkv, qkv, u)


RWKV_TB = 512
RWKV_PAIR = LANES // RWKV_HEAD_DIM
RWKV_LORA_COLS = 512


def _rwkv_kernel(r_ref, k_ref, v_ref, lo_ref, rp_ref, kp_ref, vp_ref, lop_ref,
                 mur_ref, muk_ref, muv_ref, mulo_ref,
                 w0_ref, a0_ref, kk_ref, ka_ref, rk_ref, gnw_ref, gnb_ref,
                 w2_ref, a2_ref, g2_ref, o_ref, st_ref, *, tb):
    t = pl.program_id(2)
    c = CHUNK
    hd = RWKV_HEAD_DIM

    @pl.when(t == 0)
    def _():
        st_ref[...] = jnp.zeros(st_ref.shape, F32)

    def shifted(x_ref, p_ref, mu_ref):
        x = x_ref[...]
        carry = jnp.where(t == 0, 0.0, p_ref[7:8, :])
        row = lax.broadcasted_iota(I32, x.shape, 0)
        prev = jnp.where(row == 0, carry, pltpu.roll(x, 1, axis=0))
        return x + (prev - x) * mu_ref[...]

    r = shifted(r_ref, rp_ref, mur_ref)
    k = shifted(k_ref, kp_ref, muk_ref)
    v = shifted(v_ref, vp_ref, muv_ref)
    lo = shifted(lo_ref, lop_ref, mulo_ref)
    cw, ca, cg = lo[:, :LANES], lo[:, LANES:2 * LANES], lo[:, 2 * LANES:]

    lw = -RWKV_DECAY_SCALE * jax.nn.sigmoid(w0_ref[...] + _dot(jnp.tanh(cw), w2_ref[...]))
    a = jax.nn.sigmoid(a0_ref[...] + _dot(ca, a2_ref[...]))
    g = _dot(jax.nn.sigmoid(cg), g2_ref[...])

    li = lax.broadcasted_iota(I32, (LANES, LANES), 0)
    lj = lax.broadcasted_iota(I32, (LANES, LANES), 1)
    head_ones = (li // hd == lj // hd).astype(F32)
    eye = li == lj
    strict_blk = (li % c) > (lj % c)
    incl_blk = (li % c) >= (lj % c)

    kk = k * kk_ref[...]
    kk = kk * lax.rsqrt(jnp.maximum(_dot32(kk * kk, head_ones), 1e-24))
    k2 = k * (1.0 + (a - 1.0) * ka_ref[...])
    av = -kk
    bv = kk * a
    bonus = _dot32(r * k2 * rk_ref[...], head_ones) * v

    ci = lax.broadcasted_iota(I32, (c, c), 0)
    cj = lax.broadcasted_iota(I32, (c, c), 1)
    ltri = (cj <= ci).astype(F32)
    lane = lax.broadcasted_iota(I32, (c, LANES), 1)
    head0 = lane < hd

    def stack(x):
        return jnp.concatenate([jnp.where(head0, x, 0.0), jnp.where(head0, 0.0, x)], axis=0)

    nch = range(tb // c)
    rows = [slice(n * c, (n + 1) * c) for n in nch]
    cum = [_dot32(ltri, lw[rows[n]]) for n in nch]
    tot = [cum[n][c - 1:c, :] for n in nch]
    e_neg = [jnp.exp(-cum[n]) for n in nch]
    e_end = [jnp.exp(tot[n] - cum[n]) for n in nch]
    r_s = [stack(r[rows[n]] * jnp.exp(cum[n])) for n in nch]
    a_s = [stack(av[rows[n]] * jnp.exp(cum[n] - lw[rows[n]])) for n in nch]
    b_s = [stack(bv[rows[n]] * e_neg[n]) for n in nch]
    k_s = [stack(k2[rows[n]] * e_neg[n]) for n in nch]
    bh_s = [stack(bv[rows[n]] * e_end[n]) for n in nch]
    kh_s = [stack(k2[rows[n]] * e_end[n]) for n in nch]
    v_s = [stack(v[rows[n]]) for n in nch]

    big = [_dot(jnp.concatenate([a_s[n], r_s[n]], axis=0),
                jnp.concatenate([b_s[n], k_s[n]], axis=0), _NT) for n in nch]
    nn = [jnp.where(strict_blk, big[n][:LANES, :LANES], 0.0) for n in nch]
    ak = [jnp.where(strict_blk, big[n][:LANES, LANES:], 0.0) for n in nch]
    rbk = [jnp.where(jnp.concatenate([incl_blk, incl_blk], axis=1), big[n][LANES:, :], 0.0)
           for n in nch]

    eye_f = eye.astype(F32)
    tinv = [eye_f + nn[n] for n in nch]
    npow = nn
    for _ in range(5):
        npow = [_dot(npow[n], npow[n]) for n in nch]
        tinv = [tinv[n] + _dot(tinv[n], npow[n]) for n in nch]

    akv = [_dot(ak[n], v_s[n]) for n in nch]
    p12 = [_dot(tinv[n], jnp.concatenate([a_s[n], akv[n]], axis=1)) for n in nch]
    pv = [jnp.concatenate([p12[n][:, LANES:], v_s[n]], axis=0) for n in nch]
    q1 = [r_s[n] + _dot(rbk[n][:, :LANES], p12[n][:, :LANES]) for n in nch]
    q2 = [_dot(rbk[n], pv[n]) for n in nch]
    m1 = [jnp.where(eye, jnp.exp(tot[n]), 0.0) + _dot(bh_s[n].T, p12[n][:, :LANES]) for n in nch]
    m2 = [_dot(jnp.concatenate([bh_s[n], kh_s[n]], axis=0).T, pv[n]) for n in nch]

    ys = []
    st = st_ref[...]
    for n in nch:
        both = _dot(jnp.concatenate([q1[n], m1[n]], axis=0), st)
        y_s = both[:LANES] + q2[n]
        st = both[LANES:] + m2[n]
        ys.append(y_s[:c] + y_s[c:])
    st_ref[...] = st

    y = jnp.concatenate(ys, axis=0)
    mu = _dot32(y, head_ones) * (1.0 / hd)
    yc = y - mu
    var = _dot32(yc * yc, head_ones) * (1.0 / hd)
    yn = yc * lax.rsqrt(var + RWKV_GN_EPS) * gnw_ref[...] + gnb_ref[...]
    o_ref[...] = ((yn + bonus) * g).astype(o_ref.dtype)


def rwkv7_mixer(feat, mu, w0, a0, k_k, k_a, r_k, gn_w, gn_b, w2p, a2p, g2):
    bsz, s, _ = feat.shape
    tb = min(RWKV_TB, s)
    assert s % tb == 0 and tb % CHUNK == 0
    w = RWKV_WIDTH
    npair = w // LANES
    nlo = RWKV_LORA_COLS
    lo_blk = 3 * w // nlo
    pr = tb // 8

    def cur(width, col):
        return pl.BlockSpec((None, tb, width), lambda b, p, t: (b, t, col(p)))

    def prv(width, col):
        return pl.BlockSpec((None, 8, width),
                            lambda b, p, t: (b, jnp.maximum(t * pr - 1, 0), col(p)))

    def par(arr_cols, col):
        return pl.BlockSpec((1, arr_cols), lambda b, p, t: (0, col(p)))

    cr, ck, cv = (lambda p: p), (lambda p: npair + p), (lambda p: 2 * npair + p)
    clo = lambda p: lo_blk
    vec = lambda x: x.reshape(1, -1)
    in_specs = [
        cur(LANES, cr), cur(LANES, ck), cur(LANES, cv), cur(nlo, clo),
        prv(LANES, cr), prv(LANES, ck), prv(LANES, cv), prv(nlo, clo),
        par(LANES, cr), par(LANES, ck), par(LANES, cv), par(nlo, clo),
    ] + [par(LANES, cr)] * 7 + [
        pl.BlockSpec((LANES, LANES), lambda b, p, t: (0, p)),
        pl.BlockSpec((LANES, LANES), lambda b, p, t: (0, p)),
        pl.BlockSpec((LORA_G, LANES), lambda b, p, t: (0, p)),
    ]
    return pl.pallas_call(
        functools.partial(_rwkv_kernel, tb=tb),
        out_shape=jax.ShapeDtypeStruct((bsz, s, w), BF16),
        grid=(bsz, npair, s // tb),
        in_specs=in_specs,
        out_specs=pl.BlockSpec((None, tb, LANES), lambda b, p, t: (b, t, p)),
        scratch_shapes=[pltpu.VMEM((LANES, LANES), F32)],
        compiler_params=_cparams(("parallel", "parallel", "arbitrary")),
        name="rwkv7",
    )(feat, feat, feat, feat, feat, feat, feat, feat,
      vec(mu), vec(mu), vec(mu), vec(mu),
      vec(w0), vec(a0), vec(k_k), vec(k_a), vec(r_k), vec(gn_w), vec(gn_b),
      w2p, a2p, g2)


def _pad_cols(x, n):
    return jnp.pad(x, ((0, 0), (0, n - x.shape[1])))


def _pad_rows(x, n):
    return jnp.pad(x, ((0, n - x.shape[0]), (0, 0)))


def _rwkv_cols(x):
    w = RWKV_WIDTH
    main = x[..., :3 * w]
    cw = x[..., 3 * w:3 * w + LORA_W]
    ca = x[..., 3 * w + LORA_W:3 * w + LORA_W + LORA_A]
    cg = x[..., 3 * w + LORA_W + LORA_A:]
    pad = lambda y: jnp.pad(y, [(0, 0)] * (y.ndim - 1) + [(0, LANES - y.shape[-1])])
    return jnp.concatenate([main, pad(cw), pad(ca), cg], axis=-1)


PROJ_TM = 1024
PROJ_TN = 512
ROUTER_COLS = LANES


def _router_params(rg, rgb, re, reb):
    d = rg.shape[0]
    w = jnp.zeros((d, ROUTER_COLS), F32).at[:, :N_GROUPS].set(rg)
    w = w.at[:, N_GROUPS:N_GROUPS + N_EXPERTS].set(re)
    b = jnp.zeros((ROUTER_COLS,), F32).at[:N_GROUPS].set(rgb)
    b = b.at[N_GROUPS:N_GROUPS + N_EXPERTS].set(reb)
    return w, b


def _tail(h_f32, mix_bf16, w_out, ln1_g, ln1_b, rg, rgb, re, reb, wg, wu, wd, ln2_g, ln2_b):
    mixed = matmul(mix_bf16, w_out.astype(BF16), out_dtype=F32, tm=PROJ_TM, tn=PROJ_TN)
    rw, rb = _router_params(rg, rgb, re, reb)
    h1_f32, _, logits = ln_router(h_f32, mixed, ln1_g, ln1_b, rw, rb)
    return hier_moe_ln(h1_f32, logits, wg, wu, wd, ln2_g, ln2_b)


def kernel(x, l0_w_in, l0_shift_mu, l0_w0, l0_w2, l0_a0, l0_a2, l0_g2, l0_k_k, l0_k_a, l0_r_k, l0_gn_w, l0_gn_b, l0_w_out, l0_ln1_g, l0_ln1_b, l0_router_g, l0_router_g_b, l0_router_e, l0_router_e_b, l0_w_gate, l0_w_up, l0_w_down, l0_ln2_g, l0_ln2_b, l1_w_in, l1_lq1, l1_lk1, l1_lq2, l1_lk2, l1_subln_w, l1_gate_w2, l1_gate_b, l1_gla_norm_w, l1_w_out, l1_ln1_g, l1_ln1_b, l1_router_g, l1_router_g_b, l1_router_e, l1_router_e_b, l1_w_gate, l1_w_up, l1_w_down, l1_ln2_g, l1_ln2_b):
    bsz, s, d = x.shape
    t = bsz * s
    h0 = x.reshape(t, d)

    sbw = 3 * SB_WIDTH
    x_bf16 = h0.astype(BF16)
    qkv0 = matmul(x_bf16, l0_w_in[:, :sbw].astype(BF16), out_dtype=BF16, tm=PROJ_TM, tn=PROJ_TN)
    feat = matmul(x_bf16, _rwkv_cols(l0_w_in[:, sbw:]).astype(BF16), out_dtype=F32,
                  tm=PROJ_TM, tn=PROJ_TN)
    o_sb = sb_attention(qkv0.reshape(bsz, s, sbw))
    o_rw = rwkv7_mixer(feat.reshape(bsz, s, -1), _rwkv_cols(l0_shift_mu[None])[0],
                       l0_w0, l0_a0, l0_k_k, l0_k_a, l0_r_k.reshape(-1), l0_gn_w, l0_gn_b,
                       _pad_rows(l0_w2, LANES), _pad_rows(l0_a2, LANES), l0_g2)
    mix0 = jnp.concatenate([o_sb, o_rw], axis=-1).reshape(t, d)
    h2_f32, h2_bf16 = _tail(h0, mix0, l0_w_out, l0_ln1_g, l0_ln1_b, l0_router_g, l0_router_g_b,
                            l0_router_e, l0_router_e_b, l0_w_gate, l0_w_up, l0_w_down,
                            l0_ln2_g, l0_ln2_b)

    dfw = 3 * DIFF_WIDTH
    g0 = dfw + 2 * GLA_QK + GLA_WIDTH
    g1 = g0 + GLA_GATE_RANK
    qkv1 = matmul(h2_bf16, l1_w_in[:, :dfw].astype(BF16), out_dtype=BF16, tm=PROJ_TM, tn=PROJ_TN)
    w_gla = jnp.concatenate([l1_w_in[:, dfw:g0], l1_w_in[:, g1:]], axis=1).astype(BF16)
    gproj = matmul(h2_bf16, w_gla, out_dtype=F32, tm=PROJ_TM, tn=PROJ_TN)
    cgp = matmul(h2_f32, _pad_cols(l1_w_in[:, g0:g1], LANES), out_dtype=F32, tm=512, tn=LANES,
                 exact=True)
    lam_init = 0.8 - 0.6 * math.exp(-0.3 * 1)
    lam = (jnp.exp(jnp.sum(l1_lq1 * l1_lk1)) - jnp.exp(jnp.sum(l1_lq2 * l1_lk2)) + lam_init)
    o_diff = diff_attention(qkv1.reshape(bsz, s, dfw), lam, l1_subln_w, lam_init)
    o_gla = gla_mixer(gproj.reshape(bsz, s, -1), cgp.reshape(bsz, s, LANES),
                      _pad_rows(l1_gate_w2, LANES), l1_gate_b, l1_gla_norm_w)
    mix1 = jnp.concatenate([o_diff, o_gla], axis=-1).reshape(t, d)
    out_f32, _ = _tail(h2_f32, mix1, l1_w_out, l1_ln1_g, l1_ln1_b, l1_router_g, l1_router_g_b,
                       l1_router_e, l1_router_e_b, l1_w_gate, l1_w_up, l1_w_down,
                       l1_ln2_g, l1_ln2_b)
    return out_f32.reshape(bsz, s, d)
```

```python
import functools
import math

import jax
import jax.numpy as jnp
from jax import lax
from jax.experimental import pallas as pl
from jax.experimental.pallas import tpu as pltpu

F32 = jnp.float32
BF16 = jnp.bfloat16
I32 = jnp.int32

D_MODEL = 4096
DEPTH = 2
CHUNK = 64
SB_WIDTH = 2048
SB_HEAD_DIM = 128
SB_HEADS = SB_WIDTH // SB_HEAD_DIM
RWKV_WIDTH = 2048
RWKV_HEAD_DIM = 64
LORA_W = 96
LORA_A = 96
LORA_G = 256
RWKV_DECAY_SCALE = math.exp(-0.5)
RWKV_GN_EPS = 64e-5
DIFF_WIDTH = 2048
DIFF_HEAD_DIM = 128
DIFF_HEADS = DIFF_WIDTH // (2 * DIFF_HEAD_DIM)
GLA_WIDTH = 2048
GLA_HEADS = 4
GLA_QK = GLA_WIDTH // 2
GLA_DK = GLA_QK // GLA_HEADS
GLA_DV = GLA_WIDTH // GLA_HEADS
GLA_GATE_RANK = 16
GLA_GATE_NORM = 16.0
N_GROUPS = 4
EXPERTS_PER_GROUP = 8
N_EXPERTS = N_GROUPS * EXPERTS_PER_GROUP
TOP_K = 2
D_EXPERT = 768
DEEPNORM_ALPHA = (2 * DEPTH) ** 0.25
LN_EPS = 1e-5

LANES = 128
V7X_VMEM_LIMIT = 56 * 1024 * 1024

HIGHEST = lax.Precision.HIGHEST


def _cparams(semantics, vmem=V7X_VMEM_LIMIT):
    return pltpu.CompilerParams(dimension_semantics=semantics, vmem_limit_bytes=vmem)


def _dot(a, b, dims=(((1,), (0,)), ((), ()))):
    return lax.dot_general(a.astype(BF16), b.astype(BF16), dims, preferred_element_type=F32)


def _dot32(a, b, dims=(((1,), (0,)), ((), ()))):
    return lax.dot_general(a.astype(F32), b.astype(F32), dims, precision=HIGHEST,
                           preferred_element_type=F32)


_NT = (((1,), (1,)), ((), ()))
_TN = (((0,), (0,)), ((), ()))


PROJ_TM = 1024
PROJ_TN = 512


def _proj_kernel(*refs, nx):
    x_refs, w_ref, o_ref, wb_ref = refs[:nx], refs[nx], refs[nx + 1], refs[nx + 2]

    @pl.when(pl.program_id(1) == 0)
    def _():
        wb_ref[...] = w_ref[...].astype(BF16)

    acc = None
    k0 = 0
    for x_ref in x_refs:
        kx = x_ref.shape[1]
        part = lax.dot_general(x_ref[...], wb_ref[k0:k0 + kx, :], (((1,), (0,)), ((), ())),
                               preferred_element_type=F32)
        acc = part if acc is None else acc + part
        k0 += kx
    o_ref[...] = acc.astype(o_ref.dtype)


def proj(xs, w, *, col0=0, ncols=None, out_dtype):
    m = xs[0].shape[0]
    k = w.shape[0]
    ncols = w.shape[1] - col0 if ncols is None else ncols
    tm, tn = PROJ_TM, PROJ_TN
    assert sum(x.shape[1] for x in xs) == k and m % tm == 0
    assert ncols % tn == 0 and col0 % tn == 0, (col0, ncols)
    cb = col0 // tn
    in_specs = [pl.BlockSpec((tm, x.shape[1]), lambda j, i: (i, 0)) for x in xs]
    in_specs.append(pl.BlockSpec((k, tn), lambda j, i: (0, cb + j)))
    return pl.pallas_call(
        functools.partial(_proj_kernel, nx=len(xs)),
        out_shape=jax.ShapeDtypeStruct((m, ncols), out_dtype),
        grid=(ncols // tn, m // tm),
        in_specs=in_specs,
        out_specs=pl.BlockSpec((tm, tn), lambda j, i: (i, j)),
        scratch_shapes=[pltpu.VMEM((k, tn), BF16)],
        compiler_params=_cparams(("arbitrary", "arbitrary")),
        name="proj",
    )(*xs, w)


def _proj_exact_kernel(x_ref, w_ref, o_ref):
    o_ref[...] = _dot32(x_ref[...], w_ref[...])


def proj_exact(x, w, *, tm=512):
    m, k = x.shape
    n = w.shape[1]
    return pl.pallas_call(
        _proj_exact_kernel,
        out_shape=jax.ShapeDtypeStruct((m, n), F32),
        grid=(m // tm,),
        in_specs=[pl.BlockSpec((tm, k), lambda i: (i, 0)), pl.BlockSpec((k, n), lambda i: (0, 0))],
        out_specs=pl.BlockSpec((tm, n), lambda i: (i, 0)),
        compiler_params=_cparams(("parallel",)),
        name="proj_exact",
    )(x, w)


def _layer_norm_rows(x, g, b):
    mu = jnp.mean(x, axis=-1, keepdims=True)
    xc = x - mu
    var = jnp.mean(xc * xc, axis=-1, keepdims=True)
    return xc * lax.rsqrt(var + LN_EPS) * g + b


def _ln_router_kernel(h_ref, m_ref, g_ref, b_ref, r_ref, rb_ref, of_ref, ob_ref, lg_ref):
    y = _layer_norm_rows(DEEPNORM_ALPHA * h_ref[...] + m_ref[...], g_ref[...], b_ref[...])
    of_ref[...] = y
    ob_ref[...] = y.astype(BF16)
    lg_ref[...] = _dot32(y, r_ref[...]) + rb_ref[...]


def ln_router(h, mixed, g, b, router_w, router_b, *, tm=256):
    t, d = h.shape
    nr = router_w.shape[1]
    row = lambda i: (i, 0)
    fixed = lambda i: (0, 0)
    return pl.pallas_call(
        _ln_router_kernel,
        out_shape=(jax.ShapeDtypeStruct((t, d), F32), jax.ShapeDtypeStruct((t, d), BF16),
                   jax.ShapeDtypeStruct((t, nr), F32)),
        grid=(t // tm,),
        in_specs=[pl.BlockSpec((tm, d), row), pl.BlockSpec((tm, d), row),
                  pl.BlockSpec((1, d), fixed), pl.BlockSpec((1, d), fixed),
                  pl.BlockSpec((d, nr), fixed), pl.BlockSpec((1, nr), fixed)],
        out_specs=(pl.BlockSpec((tm, d), row), pl.BlockSpec((tm, d), row),
                   pl.BlockSpec((tm, nr), row)),
        compiler_params=_cparams(("parallel",)),
        name="ln_router",
    )(h, mixed, g.reshape(1, d), b.reshape(1, d), router_w, router_b.reshape(1, nr))


MOE_ROWS = 256
MOE_UP_SPLIT = 2
MOE_TM = 256
U32 = jnp.uint32


def _pack_halves(y):
    n = y.shape[1] // 2
    bits = pltpu.bitcast(y, U32)
    rounded = bits + U32(0x7FFF) + ((bits >> 16) & U32(1))
    return (rounded[:, :n] & U32(0xFFFF0000)) | (rounded[:, n:] >> 16)


def _unpack_halves(u):
    hi = pltpu.bitcast(u & U32(0xFFFF0000), F32).astype(BF16)
    lo = pltpu.bitcast(u << 16, F32).astype(BF16)
    return hi, lo


def _dispatch_kernel(pos_ref, h_ref, xs_in, xs_out, buf, sem, *, tm, nsteps):
    del xs_in
    i = pl.program_id(0)
    slot = i % 2

    def row_copy(s, r, p):
        return pltpu.make_async_copy(buf.at[s, pl.ds(r, 1), :], xs_out.at[pl.ds(p, 1), :],
                                     sem.at[s])

    def drain(s):
        def body(r, c):
            for _ in range(TOP_K):
                row_copy(s, r, 0).wait()
            return c
        lax.fori_loop(0, tm, body, 0)

    @pl.when(i >= 2)
    def _():
        drain(slot)

    buf[slot] = _pack_halves(h_ref[...])

    def issue(r, c):
        a = (i * tm + r) * TOP_K
        for k in range(TOP_K):
            row_copy(slot, r, pos_ref[a + k]).start()
        return c
    lax.fori_loop(0, tm, issue, 0)

    @pl.when(i == nsteps - 1)
    def _():
        drain(slot)
        if nsteps > 1:
            drain(1 - slot)


def moe_dispatch(h, pos, nslots):
    t, d = h.shape
    tm = MOE_TM
    nsteps = t // tm
    grid_spec = pltpu.PrefetchScalarGridSpec(
        num_scalar_prefetch=1,
        grid=(nsteps,),
        in_specs=[pl.BlockSpec((tm, d), lambda i, pos: (i, 0)),
                  pl.BlockSpec(memory_space=pl.ANY)],
        out_specs=pl.BlockSpec(memory_space=pl.ANY),
        scratch_shapes=[pltpu.VMEM((2, tm, d // 2), U32), pltpu.SemaphoreType.DMA((2,))],
    )
    return pl.pallas_call(
        functools.partial(_dispatch_kernel, tm=tm, nsteps=nsteps),
        out_shape=jax.ShapeDtypeStruct((nslots, d // 2), U32),
        grid_spec=grid_spec,
        input_output_aliases={2: 0},
        compiler_params=_cparams(("arbitrary",)),
        name="moe_dispatch",
    )(pos.reshape(-1), h, jnp.zeros((nslots, d // 2), U32))


def _expert_changed(be_ref, i):
    return (i == 0) | (be_ref[i] != be_ref[jnp.maximum(i - 1, 0)])


def _moe_up_kernel(be_ref, nb_ref, xs_ref, wg_ref, wu_ref, a_ref, wc_ref):
    i = pl.program_id(1)
    half = xs_ref.shape[1]

    @pl.when(i < nb_ref[0])
    def _():
        @pl.when(_expert_changed(be_ref, i))
        def _():
            for hh in range(2):
                rows = slice(hh * half, (hh + 1) * half)
                wc_ref[hh] = jnp.concatenate([wg_ref[rows, :], wu_ref[rows, :]],
                                             axis=1).astype(BF16)

        hi, lo = _unpack_halves(xs_ref[...])
        hgu = _dot(hi, wc_ref[0]) + _dot(lo, wc_ref[1])
        ce = hgu.shape[1] // 2
        hg, hu = hgu[:, :ce], hgu[:, ce:]
        a_ref[...] = (hg * jax.nn.sigmoid(hg) * hu).astype(a_ref.dtype)

    @pl.when(i >= nb_ref[0])
    def _():
        a_ref[...] = jnp.zeros_like(a_ref)


def moe_up(xs, block_e, nb_used, wg, wu):
    nslots, half = xs.shape
    _, d, de = wg.shape
    rows = MOE_ROWS
    nblk = nslots // rows
    ce = de // MOE_UP_SPLIT
    last = lambda i, nb: jnp.minimum(i, nb[0] - 1)
    grid_spec = pltpu.PrefetchScalarGridSpec(
        num_scalar_prefetch=2,
        grid=(MOE_UP_SPLIT, nblk),
        in_specs=[
            pl.BlockSpec((rows, half), lambda j, i, be, nb: (last(i, nb), 0)),
            pl.BlockSpec((None, d, ce), lambda j, i, be, nb: (be[i], 0, j)),
            pl.BlockSpec((None, d, ce), lambda j, i, be, nb: (be[i], 0, j)),
        ],
        out_specs=pl.BlockSpec((rows, ce), lambda j, i, be, nb: (i, j)),
        scratch_shapes=[pltpu.VMEM((2, half, 2 * ce), BF16)],
    )
    return pl.pallas_call(
        _moe_up_kernel,
        out_shape=jax.ShapeDtypeStruct((nslots, de), BF16),
        grid_spec=grid_spec,
        compiler_params=_cparams(("arbitrary", "arbitrary")),
        name="moe_up",
    )(block_e, nb_used, xs, wg, wu)


def _moe_down_kernel(be_ref, nb_ref, a_ref, wd_ref, o_ref, wc_ref):
    i = pl.program_id(0)

    @pl.when(i < nb_ref[0])
    def _():
        @pl.when(_expert_changed(be_ref, i))
        def _():
            wc_ref[...] = wd_ref[...].astype(BF16)

        o_ref[...] = _dot(a_ref[...], wc_ref[...])

    @pl.when(i >= nb_ref[0])
    def _():
        o_ref[...] = jnp.zeros_like(o_ref)


def moe_down(act, block_e, nb_used, wd):
    nslots, de = act.shape
    d = wd.shape[2]
    rows = MOE_ROWS
    nblk = nslots // rows
    last = lambda i, nb: jnp.minimum(i, nb[0] - 1)
    grid_spec = pltpu.PrefetchScalarGridSpec(
        num_scalar_prefetch=2,
        grid=(nblk,),
        in_specs=[
            pl.BlockSpec((rows, de), lambda i, be, nb: (last(i, nb), 0)),
            pl.BlockSpec((None, de, d), lambda i, be, nb: (be[i], 0, 0)),
        ],
        out_specs=pl.BlockSpec((rows, d), lambda i, be, nb: (i, 0)),
        scratch_shapes=[pltpu.VMEM((de, d), BF16)],
    )
    return pl.pallas_call(
        _moe_down_kernel,
        out_shape=jax.ShapeDtypeStruct((nslots, d), F32),
        grid_spec=grid_spec,
        compiler_params=_cparams(("arbitrary",)),
        name="moe_down",
    )(block_e, nb_used, act, wd)


def _combine_ln_kernel(pos_ref, h_ref, gate_ref, yb_hbm, g_ref, b_ref, of_ref, ob_ref,
                       ybuf, sem, *, tm, nsteps):
    i = pl.program_id(0)
    slot = i % 2

    def row_copy(s, k, r, p):
        return pltpu.make_async_copy(yb_hbm.at[pl.ds(p, 1), :], ybuf.at[s, k, pl.ds(r, 1), :],
                                     sem.at[s])

    def issue(step, s):
        def body(r, c):
            a = (step * tm + r) * TOP_K
            for k in range(TOP_K):
                row_copy(s, k, r, pos_ref[a + k]).start()
            return c
        lax.fori_loop(0, tm, body, 0)

    @pl.when(i == 0)
    def _():
        issue(0, 0)

    @pl.when(i + 1 < nsteps)
    def _():
        issue(i + 1, 1 - slot)

    def drain(r, c):
        for k in range(TOP_K):
            row_copy(slot, k, r, 0).wait()
        return c
    lax.fori_loop(0, tm, drain, 0)

    gate = gate_ref[...]
    y = ybuf[slot, 0] * gate[:, 0:1]
    for k in range(1, TOP_K):
        y = y + ybuf[slot, k] * gate[:, k:k + 1]
    out = _layer_norm_rows(DEEPNORM_ALPHA * h_ref[...] + y, g_ref[...], b_ref[...])
    of_ref[...] = out
    ob_ref[...] = out.astype(BF16)


def combine_ln(h, gates, pos, yb, g, b):
    t, d = h.shape
    tm = MOE_TM
    nsteps = t // tm
    grid_spec = pltpu.PrefetchScalarGridSpec(
        num_scalar_prefetch=1,
        grid=(nsteps,),
        in_specs=[
            pl.BlockSpec((tm, d), lambda i, pos: (i, 0)),
            pl.BlockSpec((tm, TOP_K), lambda i, pos: (i, 0)),
            pl.BlockSpec(memory_space=pl.ANY),
            pl.BlockSpec((1, d), lambda i, pos: (0, 0)),
            pl.BlockSpec((1, d), lambda i, pos: (0, 0)),
        ],
        out_specs=(pl.BlockSpec((tm, d), lambda i, pos: (i, 0)),
                   pl.BlockSpec((tm, d), lambda i, pos: (i, 0))),
        scratch_shapes=[pltpu.VMEM((2, TOP_K, tm, d), F32), pltpu.SemaphoreType.DMA((2,))],
    )
    return pl.pallas_call(
        functools.partial(_combine_ln_kernel, tm=tm, nsteps=nsteps),
        out_shape=(jax.ShapeDtypeStruct((t, d), F32), jax.ShapeDtypeStruct((t, d), BF16)),
        grid_spec=grid_spec,
        compiler_params=_cparams(("arbitrary",)),
        name="combine_ln",
    )(pos.reshape(-1), h, gates, yb, g.reshape(1, d), b.reshape(1, d))


def _routing_tables(logits, t):
    g_logits = logits[:, :N_GROUPS]
    g_prob = jax.nn.softmax(g_logits, axis=-1)
    g_sel = jnp.argmax(g_logits, axis=-1)
    g_w = jnp.take_along_axis(g_prob, g_sel[:, None], axis=-1)[:, 0]
    e_logits = logits[:, N_GROUPS:N_GROUPS + N_EXPERTS].reshape(t, N_GROUPS, EXPERTS_PER_GROUP)
    e_in_group = jnp.take_along_axis(e_logits, g_sel[:, None, None], axis=1)[:, 0]
    top_vals, top_idx = lax.top_k(e_in_group, TOP_K)
    gates = jax.nn.softmax(top_vals, axis=-1) * g_w[:, None]
    expert_id = (g_sel[:, None] * EXPERTS_PER_GROUP + top_idx).astype(I32)

    a = t * TOP_K
    rows = MOE_ROWS
    flat_e = expert_id.reshape(a)
    onehot = (flat_e[:, None] == jnp.arange(N_EXPERTS, dtype=I32)[None, :]).astype(I32)
    seen = jnp.cumsum(onehot, axis=0)
    counts = seen[-1]
    padded = (counts + rows - 1) // rows * rows
    pad_end = jnp.cumsum(padded)
    pad_start = pad_end - padded
    pos = jnp.sum(onehot * (seen - 1 + pad_start[None, :]), axis=1).astype(I32)
    nblk = a // rows + N_EXPERTS
    block_start = jnp.arange(nblk, dtype=I32) * rows
    block_e = jnp.minimum(jnp.sum((pad_end[None, :] <= block_start[:, None]).astype(I32), axis=1),
                          N_EXPERTS - 1).astype(I32)
    nb_used = (pad_end[-1] // rows).astype(I32).reshape(1)
    return gates.astype(F32), pos.reshape(t, TOP_K), block_e, nb_used, nblk * rows


def hier_moe_ln(h_f32, logits, wg, wu, wd, ln_g, ln_b):
    t = h_f32.shape[0]
    gates, pos, block_e, nb_used, nslots = _routing_tables(logits, t)
    xs = moe_dispatch(h_f32, pos, nslots)
    act = moe_up(xs, block_e, nb_used, wg, wu)
    yb = moe_down(act, block_e, nb_used, wd)
    return combine_ln(h_f32, gates, pos, yb, ln_g, ln_b)


DIFF_T = 512
NEG_BIG = -1e30


def _lane_tile(x, n):
    return x if n == 1 else jnp.concatenate([x] * n, axis=1)


def _diff_attn_kernel(lam_ref, q1_ref, q2_ref, k1_ref, k2_ref, v_ref, sw_ref, o_ref,
                      m_ref, l_ref, acc_ref, *, t, scale, out_scale):
    i = pl.program_id(2)
    br2 = range(2)
    qs = [(q1_ref[...].astype(F32) * scale).astype(BF16),
          (q2_ref[...].astype(F32) * scale).astype(BF16)]
    k_refs = [k1_ref, k2_ref]
    m_ref[...] = jnp.full(m_ref.shape, NEG_BIG, F32)
    l_ref[...] = jnp.zeros(l_ref.shape, F32)
    acc_ref[...] = jnp.zeros(acc_ref.shape, F32)
    krep = t // LANES
    vrep = v_ref.shape[-1] // LANES

    def block(j, diagonal):
        start = pl.multiple_of(j * t, t)
        vb = v_ref[pl.ds(start, t), :]
        s = [lax.dot_general(qs[b], k_refs[b][pl.ds(start, t), :], _NT,
                             preferred_element_type=F32) for b in br2]
        if diagonal:
            row_chunk = lax.broadcasted_iota(I32, (t, t), 0) // CHUNK
            col_chunk = lax.broadcasted_iota(I32, (t, t), 1) // CHUNK
            s = [jnp.where(col_chunk <= row_chunk, s[b], NEG_BIG) for b in br2]
        m_prev = [m_ref[b] for b in br2]
        m_new = [jnp.maximum(m_prev[b], jnp.max(s[b], axis=1, keepdims=True)) for b in br2]
        p = [jnp.exp(s[b] - _lane_tile(m_new[b], krep)) for b in br2]
        alpha = [jnp.exp(m_prev[b] - m_new[b]) for b in br2]
        l_new = [alpha[b] * l_ref[b] + jnp.sum(p[b], axis=1, keepdims=True) for b in br2]
        pv = [_dot(p[b], vb) for b in br2]
        for b in br2:
            acc_ref[b] = _lane_tile(alpha[b], vrep) * acc_ref[b] + pv[b]
            m_ref[b] = m_new[b]
            l_ref[b] = l_new[b]

    def full_block(j, carry):
        block(j, False)
        return carry

    lax.fori_loop(0, i, full_block, 0)
    block(i, True)
    lam = lam_ref[0]
    inv = [_lane_tile(1.0 / l_ref[b], vrep) for b in br2]
    o = acc_ref[0] * inv[0] - lam * (acc_ref[1] * inv[1])
    ms = jnp.mean(o * o, axis=-1, keepdims=True)
    o_ref[...] = (o * lax.rsqrt(ms + 1e-5) * sw_ref[...] * out_scale).astype(o_ref.dtype)


def diff_attention(qkv, lam, subln_w, lam_init):
    bsz, s, _ = qkv.shape
    tq = min(DIFF_T, s)
    assert s % tq == 0 and tq % CHUNK == 0
    d = DIFF_HEAD_DIM
    nh = DIFF_HEADS
    kcol = DIFF_WIDTH // d
    vcol = 2 * DIFF_WIDTH // (2 * d)
    grid_spec = pltpu.PrefetchScalarGridSpec(
        num_scalar_prefetch=0,
        grid=(bsz, nh, s // tq),
        in_specs=[
            pl.BlockSpec(memory_space=pltpu.SMEM),
            pl.BlockSpec((None, tq, d), lambda b, h, i: (b, i, 2 * h)),
            pl.BlockSpec((None, tq, d), lambda b, h, i: (b, i, 2 * h + 1)),
            pl.BlockSpec((None, s, d), lambda b, h, i: (b, 0, kcol + 2 * h)),
            pl.BlockSpec((None, s, d), lambda b, h, i: (b, 0, kcol + 2 * h + 1)),
            pl.BlockSpec((None, s, 2 * d), lambda b, h, i: (b, 0, vcol + h)),
            pl.BlockSpec((1, 2 * d), lambda b, h, i: (0, 0)),
        ],
        out_specs=pl.BlockSpec((None, tq, 2 * d), lambda b, h, i: (b, i, h)),
        scratch_shapes=[pltpu.VMEM((2, tq, LANES), F32), pltpu.VMEM((2, tq, LANES), F32),
                        pltpu.VMEM((2, tq, 2 * d), F32)],
    )
    return pl.pallas_call(
        functools.partial(_diff_attn_kernel, t=tq, scale=d ** -0.5,
                          out_scale=1.0 - lam_init),
        out_shape=jax.ShapeDtypeStruct((bsz, s, DIFF_WIDTH), BF16),
        grid_spec=grid_spec,
        compiler_params=_cparams(("parallel", "parallel", "arbitrary")),
        name="diff_attn",
    )(lam.reshape(1).astype(F32), qkv, qkv, qkv, qkv, qkv, subln_w.reshape(1, 2 * d))


GLA_TB = 256


def _log_sigmoid(x):
    return jnp.minimum(x, 0.0) - jnp.log(1.0 + jnp.exp(-jnp.abs(x)))


def _gla_kernel(q_ref, k_ref, v_ref, og_ref, cg_ref, gw_ref, gb_ref, nw_ref, o_ref,
                state_ref, *, tb, scale):
    t = pl.program_id(2)

    @pl.when(t == 0)
    def _():
        state_ref[...] = jnp.zeros(state_ref.shape, F32)

    c = CHUNK
    ri = lax.broadcasted_iota(I32, (c, c), 0)
    ci = lax.broadcasted_iota(I32, (c, c), 1)
    causal = ci <= ri
    ltri = causal.astype(F32)
    ones_cols = jnp.ones((c, LANES), F32)
    dv = v_ref.shape[-1]
    for n in range(tb // c):
        rows = slice(n * c, (n + 1) * c)
        gl = _dot32(cg_ref[rows, :], gw_ref[...]) + gb_ref[...]
        log_a = _log_sigmoid(gl) * (1.0 / GLA_GATE_NORM)
        bcum = _dot32(ltri, log_a)
        b_last = bcum[c - 1:c, :]
        q = q_ref[rows, :]
        k = k_ref[rows, :]
        v = v_ref[rows, :]
        q_dec = q * scale * jnp.exp(bcum)
        k_inv = k * jnp.exp(-bcum)
        k_end = k * jnp.exp(b_last - bcum)
        scores = jnp.where(causal, _dot(q_dec, k_inv, _NT), 0.0)
        state = state_ref[...]
        o = _dot(scores, v) + _dot(q_dec, state)
        dec_col = jnp.exp(_dot32(log_a, ones_cols, _TN))
        dec = jnp.concatenate([dec_col] * (dv // LANES), axis=1)
        state_ref[...] = state * dec + _dot(k_end.T, v)
        ms = jnp.mean(o * o, axis=-1, keepdims=True)
        og = og_ref[rows, :]
        gate = og * jax.nn.sigmoid(og)
        o_ref[rows, :] = (o * lax.rsqrt(ms + 1e-5) * nw_ref[...] * gate).astype(o_ref.dtype)


def gla_mixer(proj, og, cg, gate_w2p, gate_b, norm_w):
    bsz, s, _ = proj.shape
    tb = min(GLA_TB, s)
    assert s % tb == 0 and tb % CHUNK == 0
    dk, dv, nh = GLA_DK, GLA_DV, GLA_HEADS
    in_specs = [
        pl.BlockSpec((None, tb, dk), lambda b, h, t: (b, t, h)),
        pl.BlockSpec((None, tb, dk), lambda b, h, t: (b, t, nh + h)),
        pl.BlockSpec((None, tb, dv), lambda b, h, t: (b, t, nh + h)),
        pl.BlockSpec((None, tb, dv), lambda b, h, t: (b, t, h)),
        pl.BlockSpec((None, tb, LANES), lambda b, h, t: (b, t, 0)),
        pl.BlockSpec((LANES, dk), lambda b, h, t: (0, h)),
        pl.BlockSpec((1, dk), lambda b, h, t: (0, h)),
        pl.BlockSpec((1, dv), lambda b, h, t: (0, 0)),
    ]
    return pl.pallas_call(
        functools.partial(_gla_kernel, tb=tb, scale=dk ** -0.5),
        out_shape=jax.ShapeDtypeStruct((bsz, s, GLA_WIDTH), BF16),
        grid=(bsz, nh, s // tb),
        in_specs=in_specs,
        out_specs=pl.BlockSpec((None, tb, dv), lambda b, h, t: (b, t, h)),
        scratch_shapes=[pltpu.VMEM((dk, dv), F32)],
        compiler_params=_cparams(("parallel", "parallel", "arbitrary")),
        name="gla",
    )(proj, proj, proj, og, cg, gate_w2p, gate_b.reshape(1, GLA_QK), norm_w.reshape(1, dv))


SB_T = 256
SB_HEADS_PER_STEP = 2
SB_UNDERFLOW = 110.0


def _sb_attn_kernel(q_ref, k_ref, v_ref, u_ref, o_ref, acc_ref, run_ref, *, t, nhs, scale):
    i = pl.program_id(2)
    d = SB_HEAD_DIM
    heads = range(nhs)
    cols = [slice(h * d, (h + 1) * d) for h in heads]
    q = [(q_ref[:, cols[h]].astype(F32) * scale).astype(BF16) for h in heads]
    u = u_ref[...]
    rep = t // LANES

    def block(j, diagonal, first):
        start = pl.multiple_of(j * t, t)
        kb = [k_ref[pl.ds(start, t), cols[h]] for h in heads]
        vb = [v_ref[pl.ds(start, t), cols[h]] for h in heads]
        z = [lax.dot_general(q[h], kb[h], _NT, preferred_element_type=F32) for h in heads]
        sp = [jnp.maximum(z[h], 0.0) + jnp.log(1.0 + jnp.exp(-jnp.abs(z[h]))) for h in heads]
        if diagonal:
            strict = (lax.broadcasted_iota(I32, (t, t), 1) < lax.broadcasted_iota(I32, (t, t), 0))
            sp = [jnp.where(strict, sp[h], 0.0) for h in heads]
        hi = [sp[h].astype(BF16) for h in heads]
        lo = [(sp[h] - hi[h].astype(F32)).astype(BF16) for h in heads]
        sums = [lax.dot_general(hi[h], u, (((1,), (0,)), ((), ())), preferred_element_type=F32)
                + lax.dot_general(lo[h], u, (((1,), (0,)), ((), ())), preferred_element_type=F32)
                for h in heads]
        if first:
            x = [z[h] + sums[h] for h in heads]
        else:
            x = [z[h] + sums[h] + _lane_tile(run_ref[h], rep) for h in heads]
        w = [jnp.exp(x[h]) for h in heads]
        if diagonal:
            w = [jnp.where(strict, w[h], 0.0) for h in heads]
        pv = [_dot(w[h], vb[h]) for h in heads]
        total = [jnp.broadcast_to(sums[h][:, 0:1], (t, LANES)) for h in heads]
        run_max = None
        for h in heads:
            if first:
                acc_ref[:, cols[h]] = pv[h]
                run_new = total[h]
            else:
                acc_ref[:, cols[h]] += pv[h]
                run_new = run_ref[h] + total[h]
            run_ref[h] = run_new
            m = jnp.max(run_new)
            run_max = m if run_max is None else jnp.maximum(run_max, m)
        return (run_max > -SB_UNDERFLOW).astype(I32)

    alive0 = block(i, True, True)

    def cond(c):
        return (c[0] <= i) & (c[1] > 0)

    def body(c):
        return c[0] + 1, block(i - c[0], False, False)

    lax.while_loop(cond, body, (jnp.int32(1), alive0))
    o_ref[...] = acc_ref[...].astype(o_ref.dtype)


def sb_attention(qkv):
    bsz, s, _ = qkv.shape
    t = min(SB_T, s)
    assert s % t == 0
    nhs = SB_HEADS_PER_STEP
    w = nhs * SB_HEAD_DIM
    ng = SB_HEADS // nhs
    r = lax.broadcasted_iota(I32, (t, t), 0)
    c = lax.broadcasted_iota(I32, (t, t), 1)
    u = jnp.where(r >= c, -1.0, 0.0).astype(BF16)
    return pl.pallas_call(
        functools.partial(_sb_attn_kernel, t=t, nhs=nhs, scale=SB_HEAD_DIM ** -0.5),
        out_shape=jax.ShapeDtypeStruct((bsz, s, SB_WIDTH), BF16),
        grid=(bsz, ng, s // t),
        in_specs=[
            pl.BlockSpec((None, t, w), lambda b, g, i: (b, i, g)),
            pl.BlockSpec((None, s, w), lambda b, g, i: (b, 0, ng + g)),
            pl.BlockSpec((None, s, w), lambda b, g, i: (b, 0, 2 * ng + g)),
            pl.BlockSpec((t, t), lambda b, g, i: (0, 0)),
        ],
        out_specs=pl.BlockSpec((None, t, w), lambda b, g, i: (b, i, g)),
        scratch_shapes=[pltpu.VMEM((t, w), F32), pltpu.VMEM((nhs, t, LANES), F32)],
        compiler_params=_cparams(("parallel", "parallel", "arbitrary")),
        name="sb_attn",
    )(qkv, qkv, qkv, u)


RWKV_TB = 512
RWKV_PAIR = LANES // RWKV_HEAD_DIM
RWKV_LORA_COLS = 512


def _rwkv_kernel(r_ref, k_ref, v_ref, lo_ref, rp_ref, kp_ref, vp_ref, lop_ref,
                 mur_ref, muk_ref, muv_ref, mulo_ref,
                 w0_ref, a0_ref, kk_ref, ka_ref, rk_ref, gnw_ref, gnb_ref,
                 w2_ref, a2_ref, g2_ref, o_ref, st_ref, *, tb):
    t = pl.program_id(2)
    c = CHUNK
    hd = RWKV_HEAD_DIM

    @pl.when(t == 0)
    def _():
        st_ref[...] = jnp.zeros(st_ref.shape, F32)

    def shifted(x_ref, p_ref, mu_ref):
        x = x_ref[...]
        carry = jnp.where(t == 0, 0.0, p_ref[7:8, :])
        row = lax.broadcasted_iota(I32, x.shape, 0)
        prev = jnp.where(row == 0, carry, pltpu.roll(x, 1, axis=0))
        return x + (prev - x) * mu_ref[...]

    r = shifted(r_ref, rp_ref, mur_ref)
    k = shifted(k_ref, kp_ref, muk_ref)
    v = shifted(v_ref, vp_ref, muv_ref)
    lo = shifted(lo_ref, lop_ref, mulo_ref)
    cw, ca, cg = lo[:, :LANES], lo[:, LANES:2 * LANES], lo[:, 2 * LANES:]

    lw = -RWKV_DECAY_SCALE * jax.nn.sigmoid(w0_ref[...] + _dot(jnp.tanh(cw), w2_ref[...]))
    a = jax.nn.sigmoid(a0_ref[...] + _dot(ca, a2_ref[...]))
    g = _dot(jax.nn.sigmoid(cg), g2_ref[...])

    li = lax.broadcasted_iota(I32, (LANES, LANES), 0)
    lj = lax.broadcasted_iota(I32, (LANES, LANES), 1)
    head_ones = (li // hd == lj // hd).astype(F32)
    eye = li == lj
    strict_blk = (li % c) > (lj % c)
    incl_blk = (li % c) >= (lj % c)

    kk = k * kk_ref[...]
    kk = kk * lax.rsqrt(jnp.maximum(_dot32(kk * kk, head_ones), 1e-24))
    k2 = k * (1.0 + (a - 1.0) * ka_ref[...])
    av = -kk
    bv = kk * a
    bonus = _dot32(r * k2 * rk_ref[...], head_ones) * v

    ci = lax.broadcasted_iota(I32, (c, c), 0)
    cj = lax.broadcasted_iota(I32, (c, c), 1)
    ltri = (cj <= ci).astype(F32)
    lane = lax.broadcasted_iota(I32, (c, LANES), 1)
    head0 = lane < hd

    def stack(x):
        return jnp.concatenate([jnp.where(head0, x, 0.0), jnp.where(head0, 0.0, x)], axis=0)

    nch = range(tb // c)
    rows = [slice(n * c, (n + 1) * c) for n in nch]
    cum = [_dot32(ltri, lw[rows[n]]) for n in nch]
    tot = [cum[n][c - 1:c, :] for n in nch]
    e_neg = [jnp.exp(-cum[n]) for n in nch]
    e_end = [jnp.exp(tot[n] - cum[n]) for n in nch]
    r_s = [stack(r[rows[n]] * jnp.exp(cum[n])) for n in nch]
    a_s = [stack(av[rows[n]] * jnp.exp(cum[n] - lw[rows[n]])) for n in nch]
    b_s = [stack(bv[rows[n]] * e_neg[n]) for n in nch]
    k_s = [stack(k2[rows[n]] * e_neg[n]) for n in nch]
    bh_s = [stack(bv[rows[n]] * e_end[n]) for n in nch]
    kh_s = [stack(k2[rows[n]] * e_end[n]) for n in nch]
    v_s = [stack(v[rows[n]]) for n in nch]

    big = [_dot(jnp.concatenate([a_s[n], r_s[n]], axis=0),
                jnp.concatenate([b_s[n], k_s[n]], axis=0), _NT) for n in nch]
    nn = [jnp.where(strict_blk, big[n][:LANES, :LANES], 0.0) for n in nch]
    ak = [jnp.where(strict_blk, big[n][:LANES, LANES:], 0.0) for n in nch]
    rbk = [jnp.where(jnp.concatenate([incl_blk, incl_blk], axis=1), big[n][LANES:, :], 0.0)
           for n in nch]

    eye_f = eye.astype(F32)
    tinv = [eye_f + nn[n] for n in nch]
    npow = nn
    for _ in range(5):
        npow = [_dot(npow[n], npow[n]) for n in nch]
        tinv = [tinv[n] + _dot(tinv[n], npow[n]) for n in nch]

    akv = [_dot(ak[n], v_s[n]) for n in nch]
    p12 = [_dot(tinv[n], jnp.concatenate([a_s[n], akv[n]], axis=1)) for n in nch]
    pv = [jnp.concatenate([p12[n][:, LANES:], v_s[n]], axis=0) for n in nch]
    q1 = [r_s[n] + _dot(rbk[n][:, :LANES], p12[n][:, :LANES]) for n in nch]
    q2 = [_dot(rbk[n], pv[n]) for n in nch]
    m1 = [jnp.where(eye, jnp.exp(tot[n]), 0.0) + _dot(bh_s[n].T, p12[n][:, :LANES]) for n in nch]
    m2 = [_dot(jnp.concatenate([bh_s[n], kh_s[n]], axis=0).T, pv[n]) for n in nch]

    ys = []
    st = st_ref[...]
    for n in nch:
        both = _dot(jnp.concatenate([q1[n], m1[n]], axis=0), st)
        y_s = both[:LANES] + q2[n]
        st = both[LANES:] + m2[n]
        ys.append(y_s[:c] + y_s[c:])
    st_ref[...] = st

    y = jnp.concatenate(ys, axis=0)
    mu = _dot32(y, head_ones) * (1.0 / hd)
    yc = y - mu
    var = _dot32(yc * yc, head_ones) * (1.0 / hd)
    yn = yc * lax.rsqrt(var + RWKV_GN_EPS) * gnw_ref[...] + gnb_ref[...]
    o_ref[...] = ((yn + bonus) * g).astype(o_ref.dtype)


def rwkv7_mixer(feat, lora, mu, mu_lora, w0, a0, k_k, k_a, r_k, gn_w, gn_b, w2p, a2p, g2):
    bsz, s, _ = feat.shape
    tb = min(RWKV_TB, s)
    assert s % tb == 0 and tb % CHUNK == 0
    w = RWKV_WIDTH
    npair = w // LANES
    nlo = RWKV_LORA_COLS
    pr = tb // 8

    def cur(width, col):
        return pl.BlockSpec((None, tb, width), lambda b, p, t: (b, t, col(p)))

    def prv(width, col):
        return pl.BlockSpec((None, 8, width),
                            lambda b, p, t: (b, jnp.maximum(t * pr - 1, 0), col(p)))

    def par(arr_cols, col):
        return pl.BlockSpec((1, arr_cols), lambda b, p, t: (0, col(p)))

    cr, ck, cv = (lambda p: p), (lambda p: npair + p), (lambda p: 2 * npair + p)
    clo = lambda p: 0
    vec = lambda x: x.reshape(1, -1)
    in_specs = [
        cur(LANES, cr), cur(LANES, ck), cur(LANES, cv), cur(nlo, clo),
        prv(LANES, cr), prv(LANES, ck), prv(LANES, cv), prv(nlo, clo),
        par(LANES, cr), par(LANES, ck), par(LANES, cv), par(nlo, clo),
    ] + [par(LANES, cr)] * 7 + [
        pl.BlockSpec((LANES, LANES), lambda b, p, t: (0, p)),
        pl.BlockSpec((LANES, LANES), lambda b, p, t: (0, p)),
        pl.BlockSpec((LORA_G, LANES), lambda b, p, t: (0, p)),
    ]
    return pl.pallas_call(
        functools.partial(_rwkv_kernel, tb=tb),
        out_shape=jax.ShapeDtypeStruct((bsz, s, w), BF16),
        grid=(bsz, npair, s // tb),
        in_specs=in_specs,
        out_specs=pl.BlockSpec((None, tb, LANES), lambda b, p, t: (b, t, p)),
        scratch_shapes=[pltpu.VMEM((LANES, LANES), F32)],
        compiler_params=_cparams(("parallel", "parallel", "arbitrary")),
        name="rwkv7",
    )(feat, feat, feat, lora, feat, feat, feat, lora,
      vec(mu), vec(mu), vec(mu), vec(mu_lora),
      vec(w0), vec(a0), vec(k_k), vec(k_a), vec(r_k), vec(gn_w), vec(gn_b),
      w2p, a2p, g2)


def _pad_cols(x, n):
    return jnp.pad(x, ((0, 0), (0, n - x.shape[1])))


def _pad_rows(x, n):
    return jnp.pad(x, ((0, n - x.shape[0]), (0, 0)))


def _rwkv_lora_cols(x):
    c0 = 3 * RWKV_WIDTH
    cw = x[..., c0:c0 + LORA_W]
    ca = x[..., c0 + LORA_W:c0 + LORA_W + LORA_A]
    cg = x[..., c0 + LORA_W + LORA_A:]
    pad = lambda y: jnp.pad(y, [(0, 0)] * (y.ndim - 1) + [(0, LANES - y.shape[-1])])
    return jnp.concatenate([pad(cw), pad(ca), cg], axis=-1)


ROUTER_COLS = LANES


def _router_params(rg, rgb, re, reb):
    d = rg.shape[0]
    w = jnp.zeros((d, ROUTER_COLS), F32).at[:, :N_GROUPS].set(rg)
    w = w.at[:, N_GROUPS:N_GROUPS + N_EXPERTS].set(re)
    b = jnp.zeros((ROUTER_COLS,), F32).at[:N_GROUPS].set(rgb)
    b = b.at[N_GROUPS:N_GROUPS + N_EXPERTS].set(reb)
    return w, b


def _tail(h_f32, mix_parts, w_out, ln1_g, ln1_b, rg, rgb, re, reb, wg, wu, wd, ln2_g, ln2_b):
    mixed = proj(mix_parts, w_out, out_dtype=F32)
    rw, rb = _router_params(rg, rgb, re, reb)
    h1_f32, _, logits = ln_router(h_f32, mixed, ln1_g, ln1_b, rw, rb)
    return hier_moe_ln(h1_f32, logits, wg, wu, wd, ln2_g, ln2_b)


def kernel(x, l0_w_in, l0_shift_mu, l0_w0, l0_w2, l0_a0, l0_a2, l0_g2, l0_k_k, l0_k_a, l0_r_k, l0_gn_w, l0_gn_b, l0_w_out, l0_ln1_g, l0_ln1_b, l0_router_g, l0_router_g_b, l0_router_e, l0_router_e_b, l0_w_gate, l0_w_up, l0_w_down, l0_ln2_g, l0_ln2_b, l1_w_in, l1_lq1, l1_lk1, l1_lq2, l1_lk2, l1_subln_w, l1_gate_w2, l1_gate_b, l1_gla_norm_w, l1_w_out, l1_ln1_g, l1_ln1_b, l1_router_g, l1_router_g_b, l1_router_e, l1_router_e_b, l1_w_gate, l1_w_up, l1_w_down, l1_ln2_g, l1_ln2_b):
    bsz, s, d = x.shape
    t = bsz * s
    h0 = x.reshape(t, d)

    sbw = 3 * SB_WIDTH
    rww = 3 * RWKV_WIDTH
    x_bf16 = h0.astype(BF16)
    qkv0 = proj([x_bf16], l0_w_in, col0=0, ncols=sbw, out_dtype=BF16)
    feat = proj([x_bf16], l0_w_in, col0=sbw, ncols=rww, out_dtype=F32)
    lora = proj([x_bf16], _rwkv_lora_cols(l0_w_in[:, sbw:]), out_dtype=F32)
    o_sb = sb_attention(qkv0.reshape(bsz, s, sbw))
    o_rw = rwkv7_mixer(feat.reshape(bsz, s, rww), lora.reshape(bsz, s, RWKV_LORA_COLS),
                       l0_shift_mu[:rww], _rwkv_lora_cols(l0_shift_mu[None])[0],
                       l0_w0, l0_a0, l0_k_k, l0_k_a, l0_r_k.reshape(-1), l0_gn_w, l0_gn_b,
                       _pad_rows(l0_w2, LANES), _pad_rows(l0_a2, LANES), l0_g2)
    h2_f32, h2_bf16 = _tail(h0, [o_sb.reshape(t, SB_WIDTH), o_rw.reshape(t, RWKV_WIDTH)],
                            l0_w_out, l0_ln1_g, l0_ln1_b, l0_router_g, l0_router_g_b,
                            l0_router_e, l0_router_e_b, l0_w_gate, l0_w_up, l0_w_down,
                            l0_ln2_g, l0_ln2_b)

    dfw = 3 * DIFF_WIDTH
    glw = 2 * GLA_QK + GLA_WIDTH
    g0 = dfw + glw
    g1 = g0 + GLA_GATE_RANK
    qkv1 = proj([h2_bf16], l1_w_in, col0=0, ncols=dfw, out_dtype=BF16)
    gproj = proj([h2_bf16], l1_w_in, col0=dfw, ncols=glw, out_dtype=F32)
    og = proj([h2_bf16], l1_w_in[:, g1:], out_dtype=F32)
    cgp = proj_exact(h2_f32, _pad_cols(l1_w_in[:, g0:g1], LANES))
    lam_init = 0.8 - 0.6 * math.exp(-0.3 * 1)
    lam = (jnp.exp(jnp.sum(l1_lq1 * l1_lk1)) - jnp.exp(jnp.sum(l1_lq2 * l1_lk2)) + lam_init)
    o_diff = diff_attention(qkv1.reshape(bsz, s, dfw), lam, l1_subln_w, lam_init)
    o_gla = gla_mixer(gproj.reshape(bsz, s, glw), og.reshape(bsz, s, GLA_WIDTH),
                      cgp.reshape(bsz, s, LANES), _pad_rows(l1_gate_w2, LANES), l1_gate_b,
                      l1_gla_norm_w)
    out_f32, _ = _tail(h2_f32, [o_diff.reshape(t, DIFF_WIDTH), o_gla.reshape(t, GLA_WIDTH)],
                       l1_w_out, l1_ln1_g, l1_ln1_b, l1_router_g, l1_router_g_b,
                       l1_router_e, l1_router_e_b, l1_w_gate, l1_w_up, l1_w_down,
                       l1_ln2_g, l1_ln2_b)
    return out_f32.reshape(bsz, s, d)
```

```python
import functools
import math

import jax
import jax.numpy as jnp
from jax import lax
from jax.experimental import pallas as pl
from jax.experimental.pallas import tpu as pltpu

F32 = jnp.float32
BF16 = jnp.bfloat16
I32 = jnp.int32

D_MODEL = 4096
DEPTH = 2
CHUNK = 64
SB_WIDTH = 2048
SB_HEAD_DIM = 128
SB_HEADS = SB_WIDTH // SB_HEAD_DIM
RWKV_WIDTH = 2048
RWKV_HEAD_DIM = 64
LORA_W = 96
LORA_A = 96
LORA_G = 256
RWKV_DECAY_SCALE = math.exp(-0.5)
RWKV_GN_EPS = 64e-5
DIFF_WIDTH = 2048
DIFF_HEAD_DIM = 128
DIFF_HEADS = DIFF_WIDTH // (2 * DIFF_HEAD_DIM)
GLA_WIDTH = 2048
GLA_HEADS = 4
GLA_QK = GLA_WIDTH // 2
GLA_DK = GLA_QK // GLA_HEADS
GLA_DV = GLA_WIDTH // GLA_HEADS
GLA_GATE_RANK = 16
GLA_GATE_NORM = 16.0
N_GROUPS = 4
EXPERTS_PER_GROUP = 8
N_EXPERTS = N_GROUPS * EXPERTS_PER_GROUP
TOP_K = 2
D_EXPERT = 768
DEEPNORM_ALPHA = (2 * DEPTH) ** 0.25
LN_EPS = 1e-5

LANES = 128
V7X_VMEM_LIMIT = 56 * 1024 * 1024

HIGHEST = lax.Precision.HIGHEST


def _cparams(semantics, vmem=V7X_VMEM_LIMIT):
    return pltpu.CompilerParams(dimension_semantics=semantics, vmem_limit_bytes=vmem)


def _dot(a, b, dims=(((1,), (0,)), ((), ()))):
    return lax.dot_general(a.astype(BF16), b.astype(BF16), dims, preferred_element_type=F32)


def _dot32(a, b, dims=(((1,), (0,)), ((), ()))):
    return lax.dot_general(a.astype(F32), b.astype(F32), dims, precision=HIGHEST,
                           preferred_element_type=F32)


_NT = (((1,), (1,)), ((), ()))
_TN = (((0,), (0,)), ((), ()))


PROJ_TM = 1024
PROJ_TN = 512


def _proj_kernel(*refs, nx):
    x_refs, w_ref, o_ref, wb_ref = refs[:nx], refs[nx], refs[nx + 1], refs[nx + 2]

    @pl.when(pl.program_id(1) == 0)
    def _():
        wb_ref[...] = w_ref[...].astype(BF16)

    acc = None
    k0 = 0
    for x_ref in x_refs:
        kx = x_ref.shape[1]
        part = lax.dot_general(x_ref[...], wb_ref[k0:k0 + kx, :], (((1,), (0,)), ((), ())),
                               preferred_element_type=F32)
        acc = part if acc is None else acc + part
        k0 += kx
    o_ref[...] = acc.astype(o_ref.dtype)


def proj(xs, w, *, col0=0, ncols=None, out_dtype):
    m = xs[0].shape[0]
    k = w.shape[0]
    ncols = w.shape[1] - col0 if ncols is None else ncols
    tm, tn = PROJ_TM, PROJ_TN
    assert sum(x.shape[1] for x in xs) == k and m % tm == 0
    assert ncols % tn == 0 and col0 % tn == 0, (col0, ncols)
    cb = col0 // tn
    in_specs = [pl.BlockSpec((tm, x.shape[1]), lambda j, i: (i, 0)) for x in xs]
    in_specs.append(pl.BlockSpec((k, tn), lambda j, i: (0, cb + j)))
    return pl.pallas_call(
        functools.partial(_proj_kernel, nx=len(xs)),
        out_shape=jax.ShapeDtypeStruct((m, ncols), out_dtype),
        grid=(ncols // tn, m // tm),
        in_specs=in_specs,
        out_specs=pl.BlockSpec((tm, tn), lambda j, i: (i, j)),
        scratch_shapes=[pltpu.VMEM((k, tn), BF16)],
        compiler_params=_cparams(("arbitrary", "arbitrary")),
        name="proj",
    )(*xs, w)


def _proj_exact_kernel(x_ref, w_ref, o_ref):
    o_ref[...] = _dot32(x_ref[...], w_ref[...])


def proj_exact(x, w, *, tm=512):
    m, k = x.shape
    n = w.shape[1]
    return pl.pallas_call(
        _proj_exact_kernel,
        out_shape=jax.ShapeDtypeStruct((m, n), F32),
        grid=(m // tm,),
        in_specs=[pl.BlockSpec((tm, k), lambda i: (i, 0)), pl.BlockSpec((k, n), lambda i: (0, 0))],
        out_specs=pl.BlockSpec((tm, n), lambda i: (i, 0)),
        compiler_params=_cparams(("parallel",)),
        name="proj_exact",
    )(x, w)


def _layer_norm_rows(x, g, b):
    mu = jnp.mean(x, axis=-1, keepdims=True)
    xc = x - mu
    var = jnp.mean(xc * xc, axis=-1, keepdims=True)
    return xc * lax.rsqrt(var + LN_EPS) * g + b


def _ln_router_kernel(h_ref, m_ref, g_ref, b_ref, r_ref, rb_ref, of_ref, ob_ref, lg_ref):
    y = _layer_norm_rows(DEEPNORM_ALPHA * h_ref[...] + m_ref[...], g_ref[...], b_ref[...])
    of_ref[...] = y
    ob_ref[...] = y.astype(BF16)
    lg_ref[...] = _dot32(y, r_ref[...]) + rb_ref[...]


def ln_router(h, mixed, g, b, router_w, router_b, *, tm=256):
    t, d = h.shape
    nr = router_w.shape[1]
    row = lambda i: (i, 0)
    fixed = lambda i: (0, 0)
    return pl.pallas_call(
        _ln_router_kernel,
        out_shape=(jax.ShapeDtypeStruct((t, d), F32), jax.ShapeDtypeStruct((t, d), BF16),
                   jax.ShapeDtypeStruct((t, nr), F32)),
        grid=(t // tm,),
        in_specs=[pl.BlockSpec((tm, d), row), pl.BlockSpec((tm, d), row),
                  pl.BlockSpec((1, d), fixed), pl.BlockSpec((1, d), fixed),
                  pl.BlockSpec((d, nr), fixed), pl.BlockSpec((1, nr), fixed)],
        out_specs=(pl.BlockSpec((tm, d), row), pl.BlockSpec((tm, d), row),
                   pl.BlockSpec((tm, nr), row)),
        compiler_params=_cparams(("parallel",)),
        name="ln_router",
    )(h, mixed, g.reshape(1, d), b.reshape(1, d), router_w, router_b.reshape(1, nr))


MOE_ROWS = 256
MOE_UP_SPLIT = 2
MOE_TM = 256
U32 = jnp.uint32


def _pack_halves(y):
    n = y.shape[1] // 2
    bits = pltpu.bitcast(y, U32)
    rounded = bits + U32(0x7FFF) + ((bits >> 16) & U32(1))
    return (rounded[:, :n] & U32(0xFFFF0000)) | (rounded[:, n:] >> 16)


def _unpack_halves(u):
    hi = pltpu.bitcast(u & U32(0xFFFF0000), F32).astype(BF16)
    lo = pltpu.bitcast(u << 16, F32).astype(BF16)
    return hi, lo


DMA_UNROLL = 8

ET_NEXT, ET_SLOT, ET_LAST, ET_HAS, ET_NB = 0, 1, 2, 3, 4


def _dispatch_kernel(pos_ref, et_ref, h_ref, xs_out, buf, zbuf, sem, zsem, *, tm, nsteps):
    i = pl.program_id(0)
    slot = i % 2

    @pl.when(i == 0)
    def _():
        zbuf[...] = jnp.zeros(zbuf.shape, zbuf.dtype)

        def zero_copy(e):
            first = pl.multiple_of(et_ref[ET_LAST, e], MOE_ROWS)
            return pltpu.make_async_copy(zbuf, xs_out.at[pl.ds(first, MOE_ROWS), :], zsem)

        def unused_copy(blk):
            first = pl.multiple_of(blk * MOE_ROWS, MOE_ROWS)
            return pltpu.make_async_copy(zbuf, xs_out.at[pl.ds(first, MOE_ROWS), :], zsem)

        nblk = xs_out.shape[0] // MOE_ROWS
        nb = et_ref[ET_NB, 0]
        for e in range(N_EXPERTS):
            @pl.when(et_ref[ET_HAS, e] > 0)
            def _():
                zero_copy(e).start()
        lax.fori_loop(nb, nblk, lambda blk, c: (unused_copy(blk).start(), c)[1], 0)
        for e in range(N_EXPERTS):
            @pl.when(et_ref[ET_HAS, e] > 0)
            def _():
                zero_copy(e).wait()
        lax.fori_loop(nb, nblk, lambda blk, c: (unused_copy(blk).wait(), c)[1], 0)

    def row_copy(s, r, p):
        return pltpu.make_async_copy(buf.at[s, pl.ds(r, 1), :], xs_out.at[pl.ds(p, 1), :],
                                     sem.at[s])

    def drain(s):
        def body(r, c):
            for _ in range(TOP_K):
                row_copy(s, r, 0).wait()
            return c
        lax.fori_loop(0, tm, body, 0, unroll=DMA_UNROLL)

    @pl.when(i >= 2)
    def _():
        drain(slot)

    buf[slot] = _pack_halves(h_ref[...])

    def issue(r, c):
        a = (i * tm + r) * TOP_K
        for k in range(TOP_K):
            row_copy(slot, r, pos_ref[a + k]).start()
        return c
    lax.fori_loop(0, tm, issue, 0, unroll=DMA_UNROLL)

    @pl.when(i == nsteps - 1)
    def _():
        drain(slot)
        if nsteps > 1:
            drain(1 - slot)


def moe_dispatch(h, pos, etab, nslots):
    t, d = h.shape
    tm = MOE_TM
    nsteps = t // tm
    grid_spec = pltpu.PrefetchScalarGridSpec(
        num_scalar_prefetch=2,
        grid=(nsteps,),
        in_specs=[pl.BlockSpec((tm, d), lambda i, pos, et: (i, 0))],
        out_specs=pl.BlockSpec(memory_space=pl.ANY),
        scratch_shapes=[pltpu.VMEM((2, tm, d // 2), U32), pltpu.VMEM((MOE_ROWS, d // 2), U32),
                        pltpu.SemaphoreType.DMA((2,)), pltpu.SemaphoreType.DMA(())],
    )
    return pl.pallas_call(
        functools.partial(_dispatch_kernel, tm=tm, nsteps=nsteps),
        out_shape=jax.ShapeDtypeStruct((nslots, d // 2), U32),
        grid_spec=grid_spec,
        compiler_params=_cparams(("arbitrary",)),
        name="moe_dispatch",
    )(pos.reshape(-1), etab, h)


def _expert_changed(be_ref, i):
    return (i == 0) | (be_ref[i] != be_ref[jnp.maximum(i - 1, 0)])


def _expert_weights(be_ref, et_ref, i, copies, convert):
    e = be_ref[i]
    slot = et_ref[ET_SLOT, e]

    @pl.when(_expert_changed(be_ref, i))
    def _():
        @pl.when(i == 0)
        def _():
            for c in copies(e, slot):
                c.start()
        for c in copies(e, slot):
            c.wait()
        nxt = et_ref[ET_NEXT, e]

        @pl.when(nxt >= 0)
        def _():
            for c in copies(nxt, 1 - slot):
                c.start()
        convert(slot)


def _moe_up_kernel(be_ref, nb_ref, et_ref, xs_ref, wg_hbm, wu_hbm, a_ref, wst, wc_ref, sem):
    j = pl.program_id(0)
    i = pl.program_id(1)
    half = xs_ref.shape[1]
    ce = a_ref.shape[1]

    def copies(e, slot):
        cols = pl.ds(pl.multiple_of(j * ce, LANES), ce)
        return [pltpu.make_async_copy(wg_hbm.at[e, :, cols], wst.at[slot, 0], sem.at[slot]),
                pltpu.make_async_copy(wu_hbm.at[e, :, cols], wst.at[slot, 1], sem.at[slot])]

    def convert(slot):
        for hh in range(2):
            rows = slice(hh * half, (hh + 1) * half)
            wc_ref[hh] = jnp.concatenate([wst[slot, 0, rows, :], wst[slot, 1, rows, :]],
                                         axis=1).astype(BF16)

    @pl.when(i < nb_ref[0])
    def _():
        _expert_weights(be_ref, et_ref, i, copies, convert)
        hi, lo = _unpack_halves(xs_ref[...])
        hgu = _dot(hi, wc_ref[0]) + _dot(lo, wc_ref[1])
        hg, hu = hgu[:, :ce], hgu[:, ce:]
        a_ref[...] = (hg * jax.nn.sigmoid(hg) * hu).astype(a_ref.dtype)

    @pl.when(i >= nb_ref[0])
    def _():
        a_ref[...] = jnp.zeros_like(a_ref)


def moe_up(xs, block_e, nb_used, etab, wg, wu):
    nslots, half = xs.shape
    _, d, de = wg.shape
    rows = MOE_ROWS
    nblk = nslots // rows
    ce = de // MOE_UP_SPLIT
    last = lambda i, nb: jnp.minimum(i, nb[0] - 1)
    grid_spec = pltpu.PrefetchScalarGridSpec(
        num_scalar_prefetch=3,
        grid=(MOE_UP_SPLIT, nblk),
        in_specs=[
            pl.BlockSpec((rows, half), lambda j, i, be, nb, et: (last(i, nb), 0)),
            pl.BlockSpec(memory_space=pl.ANY),
            pl.BlockSpec(memory_space=pl.ANY),
        ],
        out_specs=pl.BlockSpec((rows, ce), lambda j, i, be, nb, et: (i, j)),
        scratch_shapes=[pltpu.VMEM((2, 2, d, ce), F32), pltpu.VMEM((2, half, 2 * ce), BF16),
                        pltpu.SemaphoreType.DMA((2,))],
    )
    return pl.pallas_call(
        _moe_up_kernel,
        out_shape=jax.ShapeDtypeStruct((nslots, de), BF16),
        grid_spec=grid_spec,
        compiler_params=_cparams(("arbitrary", "arbitrary")),
        name="moe_up",
    )(block_e, nb_used, etab, xs, wg, wu)


def _moe_down_kernel(be_ref, nb_ref, et_ref, a_ref, wd_hbm, o_ref, wst, wc_ref, sem):
    i = pl.program_id(0)

    def copies(e, slot):
        return [pltpu.make_async_copy(wd_hbm.at[e], wst.at[slot], sem.at[slot])]

    def convert(slot):
        wc_ref[...] = wst[slot].astype(BF16)

    @pl.when(i < nb_ref[0])
    def _():
        _expert_weights(be_ref, et_ref, i, copies, convert)
        o_ref[...] = _dot(a_ref[...], wc_ref[...])

    @pl.when(i >= nb_ref[0])
    def _():
        o_ref[...] = jnp.zeros_like(o_ref)


def moe_down(act, block_e, nb_used, etab, wd):
    nslots, de = act.shape
    d = wd.shape[2]
    rows = MOE_ROWS
    nblk = nslots // rows
    last = lambda i, nb: jnp.minimum(i, nb[0] - 1)
    grid_spec = pltpu.PrefetchScalarGridSpec(
        num_scalar_prefetch=3,
        grid=(nblk,),
        in_specs=[
            pl.BlockSpec((rows, de), lambda i, be, nb, et: (last(i, nb), 0)),
            pl.BlockSpec(memory_space=pl.ANY),
        ],
        out_specs=pl.BlockSpec((rows, d), lambda i, be, nb, et: (i, 0)),
        scratch_shapes=[pltpu.VMEM((2, de, d), F32), pltpu.VMEM((de, d), BF16),
                        pltpu.SemaphoreType.DMA((2,))],
    )
    return pl.pallas_call(
        _moe_down_kernel,
        out_shape=jax.ShapeDtypeStruct((nslots, d), F32),
        grid_spec=grid_spec,
        compiler_params=_cparams(("arbitrary",)),
        name="moe_down",
    )(block_e, nb_used, etab, act, wd)


def _combine_ln_kernel(pos_ref, h_ref, gate_ref, yb_hbm, g_ref, b_ref, of_ref, ob_ref,
                       ybuf, sem, *, tm, nsteps):
    i = pl.program_id(0)
    slot = i % 2

    def row_copy(s, k, r, p):
        return pltpu.make_async_copy(yb_hbm.at[pl.ds(p, 1), :], ybuf.at[s, k, pl.ds(r, 1), :],
                                     sem.at[s])

    def issue(step, s):
        def body(r, c):
            a = (step * tm + r) * TOP_K
            for k in range(TOP_K):
                row_copy(s, k, r, pos_ref[a + k]).start()
            return c
        lax.fori_loop(0, tm, body, 0, unroll=DMA_UNROLL)

    @pl.when(i == 0)
    def _():
        issue(0, 0)

    @pl.when(i + 1 < nsteps)
    def _():
        issue(i + 1, 1 - slot)

    def drain(r, c):
        for k in range(TOP_K):
            row_copy(slot, k, r, 0).wait()
        return c
    lax.fori_loop(0, tm, drain, 0, unroll=DMA_UNROLL)

    gate = gate_ref[...]
    y = ybuf[slot, 0] * gate[:, 0:1]
    for k in range(1, TOP_K):
        y = y + ybuf[slot, k] * gate[:, k:k + 1]
    out = _layer_norm_rows(DEEPNORM_ALPHA * h_ref[...] + y, g_ref[...], b_ref[...])
    of_ref[...] = out
    ob_ref[...] = out.astype(BF16)


def combine_ln(h, gates, pos, yb, g, b):
    t, d = h.shape
    tm = MOE_TM
    nsteps = t // tm
    grid_spec = pltpu.PrefetchScalarGridSpec(
        num_scalar_prefetch=1,
        grid=(nsteps,),
        in_specs=[
            pl.BlockSpec((tm, d), lambda i, pos: (i, 0)),
            pl.BlockSpec((tm, TOP_K), lambda i, pos: (i, 0)),
            pl.BlockSpec(memory_space=pl.ANY),
            pl.BlockSpec((1, d), lambda i, pos: (0, 0)),
            pl.BlockSpec((1, d), lambda i, pos: (0, 0)),
        ],
        out_specs=(pl.BlockSpec((tm, d), lambda i, pos: (i, 0)),
                   pl.BlockSpec((tm, d), lambda i, pos: (i, 0))),
        scratch_shapes=[pltpu.VMEM((2, TOP_K, tm, d), F32), pltpu.SemaphoreType.DMA((2,))],
    )
    return pl.pallas_call(
        functools.partial(_combine_ln_kernel, tm=tm, nsteps=nsteps),
        out_shape=(jax.ShapeDtypeStruct((t, d), F32), jax.ShapeDtypeStruct((t, d), BF16)),
        grid_spec=grid_spec,
        compiler_params=_cparams(("arbitrary",)),
        name="combine_ln",
    )(pos.reshape(-1), h, gates, yb, g.reshape(1, d), b.reshape(1, d))


def _routing_tables(logits, t):
    g_logits = logits[:, :N_GROUPS]
    g_prob = jax.nn.softmax(g_logits, axis=-1)
    g_sel = jnp.argmax(g_logits, axis=-1)
    g_w = jnp.take_along_axis(g_prob, g_sel[:, None], axis=-1)[:, 0]
    e_logits = logits[:, N_GROUPS:N_GROUPS + N_EXPERTS].reshape(t, N_GROUPS, EXPERTS_PER_GROUP)
    e_in_group = jnp.take_along_axis(e_logits, g_sel[:, None, None], axis=1)[:, 0]
    top_vals, top_idx = lax.top_k(e_in_group, TOP_K)
    gates = jax.nn.softmax(top_vals, axis=-1) * g_w[:, None]
    expert_id = (g_sel[:, None] * EXPERTS_PER_GROUP + top_idx).astype(I32)

    a = t * TOP_K
    rows = MOE_ROWS
    flat_e = expert_id.reshape(a)
    onehot = (flat_e[:, None] == jnp.arange(N_EXPERTS, dtype=I32)[None, :]).astype(I32)
    seen = jnp.cumsum(onehot, axis=0)
    counts = seen[-1]
    padded = (counts + rows - 1) // rows * rows
    pad_end = jnp.cumsum(padded)
    pad_start = pad_end - padded
    pos = jnp.sum(onehot * (seen - 1 + pad_start[None, :]), axis=1).astype(I32)
    nblk = a // rows + N_EXPERTS
    block_start = jnp.arange(nblk, dtype=I32) * rows
    block_e = jnp.minimum(jnp.sum((pad_end[None, :] <= block_start[:, None]).astype(I32), axis=1),
                          N_EXPERTS - 1).astype(I32)
    nb_used = (pad_end[-1] // rows).astype(I32).reshape(1)
    eidx = jnp.arange(N_EXPERTS, dtype=I32)
    has = counts > 0
    from_here = lax.cummin(jnp.where(has, eidx, N_EXPERTS), reverse=True)
    nxt = jnp.concatenate([from_here[1:], jnp.full((1,), N_EXPERTS, I32)])
    nxt = jnp.where(nxt >= N_EXPERTS, -1, nxt)
    run_slot = (jnp.cumsum(has.astype(I32)) - has.astype(I32)) % 2
    last_first = jnp.where(has, pad_end - rows, 0)
    etab = jnp.stack([nxt, run_slot, last_first, has.astype(I32),
                      jnp.broadcast_to(nb_used, (N_EXPERTS,))]).astype(I32)
    return gates.astype(F32), pos.reshape(t, TOP_K), block_e, nb_used, etab, nblk * rows


def hier_moe_ln(h_f32, logits, wg, wu, wd, ln_g, ln_b):
    t = h_f32.shape[0]
    gates, pos, block_e, nb_used, etab, nslots = _routing_tables(logits, t)
    xs = moe_dispatch(h_f32, pos, etab, nslots)
    act = moe_up(xs, block_e, nb_used, etab, wg, wu)
    yb = moe_down(act, block_e, nb_used, etab, wd)
    return combine_ln(h_f32, gates, pos, yb, ln_g, ln_b)


DIFF_T = 512
NEG_BIG = -1e30


def _lane_tile(x, n):
    return x if n == 1 else jnp.concatenate([x] * n, axis=1)


DIFF_HEADS_PER_STEP = 2


def _diff_attn_kernel(lam_ref, q_ref, k_ref, v_ref, sw_ref, o_ref,
                      m_ref, l_ref, acc_ref, *, t, nhs, scale, out_scale):
    i = pl.program_id(2)
    d = DIFF_HEAD_DIM
    chains = range(2 * nhs)
    qk_cols = [slice(c * d, (c + 1) * d) for c in chains]
    v_cols = [slice(c // 2 * 2 * d, (c // 2 + 1) * 2 * d) for c in chains]
    qs = [(q_ref[:, qk_cols[c]].astype(F32) * scale).astype(BF16) for c in chains]
    m_ref[...] = jnp.full(m_ref.shape, NEG_BIG, F32)
    l_ref[...] = jnp.zeros(l_ref.shape, F32)
    acc_ref[...] = jnp.zeros(acc_ref.shape, F32)
    krep = t // LANES
    vrep = 2 * d // LANES

    def block(j, diagonal):
        start = pl.multiple_of(j * t, t)
        s = [lax.dot_general(qs[c], k_ref[pl.ds(start, t), qk_cols[c]], _NT,
                             preferred_element_type=F32) for c in chains]
        if diagonal:
            row_chunk = lax.broadcasted_iota(I32, (t, t), 0) // CHUNK
            col_chunk = lax.broadcasted_iota(I32, (t, t), 1) // CHUNK
            s = [jnp.where(col_chunk <= row_chunk, s[c], NEG_BIG) for c in chains]
        m_prev = [m_ref[c] for c in chains]
        m_new = [jnp.maximum(m_prev[c], jnp.max(s[c], axis=1, keepdims=True)) for c in chains]
        p = [jnp.exp(s[c] - _lane_tile(m_new[c], krep)) for c in chains]
        alpha = [jnp.exp(m_prev[c] - m_new[c]) for c in chains]
        l_new = [alpha[c] * l_ref[c] + jnp.sum(p[c], axis=1, keepdims=True) for c in chains]
        pv = [_dot(p[c], v_ref[pl.ds(start, t), v_cols[c]]) for c in chains]
        for c in chains:
            acc_ref[c] = _lane_tile(alpha[c], vrep) * acc_ref[c] + pv[c]
            m_ref[c] = m_new[c]
            l_ref[c] = l_new[c]

    def full_block(j, carry):
        block(j, False)
        return carry

    lax.fori_loop(0, i, full_block, 0)
    block(i, True)
    lam = lam_ref[0]
    for h in range(nhs):
        inv = [_lane_tile(1.0 / l_ref[2 * h + b], vrep) for b in range(2)]
        o = acc_ref[2 * h] * inv[0] - lam * (acc_ref[2 * h + 1] * inv[1])
        ms = jnp.mean(o * o, axis=-1, keepdims=True)
        o_ref[:, v_cols[2 * h]] = (o * lax.rsqrt(ms + 1e-5) * sw_ref[...]
                                   * out_scale).astype(o_ref.dtype)


def diff_attention(qkv, lam, subln_w, lam_init):
    bsz, s, _ = qkv.shape
    tq = min(DIFF_T, s)
    assert s % tq == 0 and tq % CHUNK == 0
    d = DIFF_HEAD_DIM
    nhs = DIFF_HEADS_PER_STEP
    w = nhs * 2 * d
    ng = DIFF_WIDTH // w
    grid_spec = pltpu.PrefetchScalarGridSpec(
        num_scalar_prefetch=0,
        grid=(bsz, ng, s // tq),
        in_specs=[
            pl.BlockSpec(memory_space=pltpu.SMEM),
            pl.BlockSpec((None, tq, w), lambda b, g, i: (b, i, g)),
            pl.BlockSpec((None, s, w), lambda b, g, i: (b, 0, ng + g)),
            pl.BlockSpec((None, s, w), lambda b, g, i: (b, 0, 2 * ng + g)),
            pl.BlockSpec((1, 2 * d), lambda b, g, i: (0, 0)),
        ],
        out_specs=pl.BlockSpec((None, tq, w), lambda b, g, i: (b, i, g)),
        scratch_shapes=[pltpu.VMEM((2 * nhs, tq, LANES), F32),
                        pltpu.VMEM((2 * nhs, tq, LANES), F32),
                        pltpu.VMEM((2 * nhs, tq, 2 * d), F32)],
    )
    return pl.pallas_call(
        functools.partial(_diff_attn_kernel, t=tq, nhs=nhs, scale=d ** -0.5,
                          out_scale=1.0 - lam_init),
        out_shape=jax.ShapeDtypeStruct((bsz, s, DIFF_WIDTH), BF16),
        grid_spec=grid_spec,
        compiler_params=_cparams(("parallel", "parallel", "arbitrary")),
        name="diff_attn",
    )(lam.reshape(1).astype(F32), qkv, qkv, qkv, subln_w.reshape(1, 2 * d))


GLA_TB = 512


def _log_sigmoid(x):
    return jnp.minimum(x, 0.0) - jnp.log(1.0 + jnp.exp(-jnp.abs(x)))


def _gla_kernel(q_ref, k_ref, v_ref, og_ref, cg_ref, gw_ref, gb_ref, nw_ref, o_ref,
                state_ref, *, tb, scale):
    t = pl.program_id(2)

    @pl.when(t == 0)
    def _():
        state_ref[...] = jnp.zeros(state_ref.shape, F32)

    c = CHUNK
    ri = lax.broadcasted_iota(I32, (c, c), 0)
    ci = lax.broadcasted_iota(I32, (c, c), 1)
    causal = ci <= ri
    ltri = causal.astype(F32)
    ones_cols = jnp.ones((c, LANES), F32)
    dv = v_ref.shape[-1]
    nch = range(tb // c)
    rows = [slice(n * c, (n + 1) * c) for n in nch]
    gl = [_dot32(cg_ref[rows[n], :], gw_ref[...]) + gb_ref[...] for n in nch]
    log_a = [_log_sigmoid(gl[n]) * (1.0 / GLA_GATE_NORM) for n in nch]
    bcum = [_dot32(ltri, log_a[n]) for n in nch]
    v = [v_ref[rows[n], :] for n in nch]
    q_dec = [q_ref[rows[n], :] * scale * jnp.exp(bcum[n]) for n in nch]
    k_inv = [k_ref[rows[n], :] * jnp.exp(-bcum[n]) for n in nch]
    k_end = [k_ref[rows[n], :] * jnp.exp(bcum[n][c - 1:c, :] - bcum[n]) for n in nch]
    scores = [jnp.where(causal, _dot(q_dec[n], k_inv[n], _NT), 0.0) for n in nch]
    o_intra = [_dot(scores[n], v[n]) for n in nch]
    dec = [_lane_tile(jnp.exp(_dot32(log_a[n], ones_cols, _TN)), dv // LANES) for n in nch]
    kv = [_dot(k_end[n].T, v[n]) for n in nch]

    state = state_ref[...]
    o = []
    for n in nch:
        o.append(o_intra[n] + _dot(q_dec[n], state))
        state = state * dec[n] + kv[n]
    state_ref[...] = state

    for n in nch:
        ms = jnp.mean(o[n] * o[n], axis=-1, keepdims=True)
        og = og_ref[rows[n], :]
        gate = og * jax.nn.sigmoid(og)
        o_ref[rows[n], :] = (o[n] * lax.rsqrt(ms + 1e-5) * nw_ref[...] * gate).astype(o_ref.dtype)


def gla_mixer(proj, og, cg, gate_w2p, gate_b, norm_w):
    bsz, s, _ = proj.shape
    tb = min(GLA_TB, s)
    assert s % tb == 0 and tb % CHUNK == 0
    dk, dv, nh = GLA_DK, GLA_DV, GLA_HEADS
    in_specs = [
        pl.BlockSpec((None, tb, dk), lambda b, h, t: (b, t, h)),
        pl.BlockSpec((None, tb, dk), lambda b, h, t: (b, t, nh + h)),
        pl.BlockSpec((None, tb, dv), lambda b, h, t: (b, t, nh + h)),
        pl.BlockSpec((None, tb, dv), lambda b, h, t: (b, t, h)),
        pl.BlockSpec((None, tb, LANES), lambda b, h, t: (b, t, 0)),
        pl.BlockSpec((LANES, dk), lambda b, h, t: (0, h)),
        pl.BlockSpec((1, dk), lambda b, h, t: (0, h)),
        pl.BlockSpec((1, dv), lambda b, h, t: (0, 0)),
    ]
    return pl.pallas_call(
        functools.partial(_gla_kernel, tb=tb, scale=dk ** -0.5),
        out_shape=jax.ShapeDtypeStruct((bsz, s, GLA_WIDTH), BF16),
        grid=(bsz, nh, s // tb),
        in_specs=in_specs,
        out_specs=pl.BlockSpec((None, tb, dv), lambda b, h, t: (b, t, h)),
        scratch_shapes=[pltpu.VMEM((dk, dv), F32)],
        compiler_params=_cparams(("parallel", "parallel", "arbitrary")),
        name="gla",
    )(proj, proj, proj, og, cg, gate_w2p, gate_b.reshape(1, GLA_QK), norm_w.reshape(1, dv))


SB_T = 256
SB_HEADS_PER_STEP = 2
SB_UNDERFLOW = 110.0


def _sb_attn_kernel(q_ref, k_ref, v_ref, u_ref, o_ref, acc_ref, run_ref, *, t, nhs, scale):
    i = pl.program_id(2)
    d = SB_HEAD_DIM
    heads = range(nhs)
    cols = [slice(h * d, (h + 1) * d) for h in heads]
    q = [(q_ref[:, cols[h]].astype(F32) * scale).astype(BF16) for h in heads]
    u = u_ref[...]
    rep = t // LANES

    def block(j, diagonal, first):
        start = pl.multiple_of(j * t, t)
        kb = [k_ref[pl.ds(start, t), cols[h]] for h in heads]
        vb = [v_ref[pl.ds(start, t), cols[h]] for h in heads]
        z = [lax.dot_general(q[h], kb[h], _NT, preferred_element_type=F32) for h in heads]
        sp = [jnp.maximum(z[h], 0.0) + jnp.log(1.0 + jnp.exp(-jnp.abs(z[h]))) for h in heads]
        if diagonal:
            strict = (lax.broadcasted_iota(I32, (t, t), 1) < lax.broadcasted_iota(I32, (t, t), 0))
            sp = [jnp.where(strict, sp[h], 0.0) for h in heads]
        hi = [sp[h].astype(BF16) for h in heads]
        lo = [(sp[h] - hi[h].astype(F32)).astype(BF16) for h in heads]
        sums = [lax.dot_general(hi[h], u, (((1,), (0,)), ((), ())), preferred_element_type=F32)
                + lax.dot_general(lo[h], u, (((1,), (0,)), ((), ())), preferred_element_type=F32)
                for h in heads]
        if first:
            x = [z[h] + sums[h] for h in heads]
        else:
            x = [z[h] + sums[h] + _lane_tile(run_ref[h], rep) for h in heads]
        w = [jnp.exp(x[h]) for h in heads]
        if diagonal:
            w = [jnp.where(strict, w[h], 0.0) for h in heads]
        pv = [_dot(w[h], vb[h]) for h in heads]
        total = [jnp.broadcast_to(sums[h][:, 0:1], (t, LANES)) for h in heads]
        run_max = None
        for h in heads:
            if first:
                acc_ref[:, cols[h]] = pv[h]
                run_new = total[h]
            else:
                acc_ref[:, cols[h]] += pv[h]
                run_new = run_ref[h] + total[h]
            run_ref[h] = run_new
            m = jnp.max(run_new)
            run_max = m if run_max is None else jnp.maximum(run_max, m)
        return (run_max > -SB_UNDERFLOW).astype(I32)

    alive0 = block(i, True, True)

    def cond(c):
        return (c[0] <= i) & (c[1] > 0)

    def body(c):
        return c[0] + 1, block(i - c[0], False, False)

    lax.while_loop(cond, body, (jnp.int32(1), alive0))
    o_ref[...] = acc_ref[...].astype(o_ref.dtype)


def sb_attention(qkv):
    bsz, s, _ = qkv.shape
    t = min(SB_T, s)
    assert s % t == 0
    nhs = SB_HEADS_PER_STEP
    w = nhs * SB_HEAD_DIM
    ng = SB_HEADS // nhs
    r = lax.broadcasted_iota(I32, (t, t), 0)
    c = lax.broadcasted_iota(I32, (t, t), 1)
    u = jnp.where(r >= c, -1.0, 0.0).astype(BF16)
    return pl.pallas_call(
        functools.partial(_sb_attn_kernel, t=t, nhs=nhs, scale=SB_HEAD_DIM ** -0.5),
        out_shape=jax.ShapeDtypeStruct((bsz, s, SB_WIDTH), BF16),
        grid=(bsz, ng, s // t),
        in_specs=[
            pl.BlockSpec((None, t, w), lambda b, g, i: (b, i, g)),
            pl.BlockSpec((None, s, w), lambda b, g, i: (b, 0, ng + g)),
            pl.BlockSpec((None, s, w), lambda b, g, i: (b, 0, 2 * ng + g)),
            pl.BlockSpec((t, t), lambda b, g, i: (0, 0)),
        ],
        out_specs=pl.BlockSpec((None, t, w), lambda b, g, i: (b, i, g)),
        scratch_shapes=[pltpu.VMEM((t, w), F32), pltpu.VMEM((nhs, t, LANES), F32)],
        compiler_params=_cparams(("parallel", "parallel", "arbitrary")),
        name="sb_attn",
    )(qkv, qkv, qkv, u)


RWKV_TB = 512
RWKV_PAIR = LANES // RWKV_HEAD_DIM
RWKV_LORA_COLS = 512


def _rwkv_kernel(r_ref, k_ref, v_ref, lo_ref, rp_ref, kp_ref, vp_ref, lop_ref,
                 mur_ref, muk_ref, muv_ref, mulo_ref,
                 w0_ref, a0_ref, kk_ref, ka_ref, rk_ref, gnw_ref, gnb_ref,
                 w2_ref, a2_ref, g2_ref, o_ref, st_ref, *, tb):
    t = pl.program_id(2)
    c = CHUNK
    hd = RWKV_HEAD_DIM
    nch = range(tb // c)

    @pl.when(t == 0)
    def _():
        st_ref[...] = jnp.zeros(st_ref.shape, F32)

    li = lax.broadcasted_iota(I32, (LANES, LANES), 0)
    lj = lax.broadcasted_iota(I32, (LANES, LANES), 1)
    head_ones = (li // hd == lj // hd).astype(F32)

    def shifted(x_ref, p_ref, mu_ref):
        x = x_ref[...]
        carry = jnp.where(t == 0, 0.0, p_ref[7:8, :])
        row = lax.broadcasted_iota(I32, x.shape, 0)
        prev = jnp.where(row == 0, carry, pltpu.roll(x, 1, axis=0))
        return x + (prev - x) * mu_ref[...]

    r = shifted(r_ref, rp_ref, mur_ref)
    k = shifted(k_ref, kp_ref, muk_ref)
    v = shifted(v_ref, vp_ref, muv_ref)
    lo = shifted(lo_ref, lop_ref, mulo_ref)
    cw, ca, cg = lo[:, :LANES], lo[:, LANES:2 * LANES], lo[:, 2 * LANES:]

    lw = -RWKV_DECAY_SCALE * jax.nn.sigmoid(w0_ref[...] + _dot(jnp.tanh(cw), w2_ref[...]))
    a = jax.nn.sigmoid(a0_ref[...] + _dot(ca, a2_ref[...]))
    g = _dot(jax.nn.sigmoid(cg), g2_ref[...])

    eye = li == lj
    strict_blk = (li % c) > (lj % c)
    incl_blk = (li % c) >= (lj % c)

    kk = k * kk_ref[...]
    kk = kk * lax.rsqrt(jnp.maximum(_dot32(kk * kk, head_ones), 1e-24))
    k2 = k * (1.0 + (a - 1.0) * ka_ref[...])
    av = -kk
    bv = kk * a
    bonus = _dot32(r * k2 * rk_ref[...], head_ones) * v

    ci = lax.broadcasted_iota(I32, (c, c), 0)
    cj = lax.broadcasted_iota(I32, (c, c), 1)
    ltri = (cj <= ci).astype(F32)
    lane = lax.broadcasted_iota(I32, (c, LANES), 1)
    head0 = lane < hd

    def stack(x):
        return jnp.concatenate([jnp.where(head0, x, 0.0), jnp.where(head0, 0.0, x)], axis=0)

    rows = [slice(n * c, (n + 1) * c) for n in nch]
    cum = [_dot32(ltri, lw[rows[n]]) for n in nch]
    tot = [cum[n][c - 1:c, :] for n in nch]
    e_neg = [jnp.exp(-cum[n]) for n in nch]
    e_end = [jnp.exp(tot[n] - cum[n]) for n in nch]
    r_s = [stack(r[rows[n]] * jnp.exp(cum[n])) for n in nch]
    a_s = [stack(av[rows[n]] * jnp.exp(cum[n] - lw[rows[n]])) for n in nch]
    b_s = [stack(bv[rows[n]] * e_neg[n]) for n in nch]
    k_s = [stack(k2[rows[n]] * e_neg[n]) for n in nch]
    bh_s = [stack(bv[rows[n]] * e_end[n]) for n in nch]
    kh_s = [stack(k2[rows[n]] * e_end[n]) for n in nch]
    v_s = [stack(v[rows[n]]) for n in nch]

    big = [_dot(jnp.concatenate([a_s[n], r_s[n]], axis=0),
                jnp.concatenate([b_s[n], k_s[n]], axis=0), _NT) for n in nch]
    nn = [jnp.where(strict_blk, big[n][:LANES, :LANES], 0.0) for n in nch]
    ak = [jnp.where(strict_blk, big[n][:LANES, LANES:], 0.0) for n in nch]
    rbk = [jnp.where(jnp.concatenate([incl_blk, incl_blk], axis=1), big[n][LANES:, :], 0.0)
           for n in nch]

    eye_f = eye.astype(F32)
    tinv = [eye_f + nn[n] for n in nch]
    npow = nn
    for _ in range(5):
        npow = [_dot(npow[n], npow[n]) for n in nch]
        tinv = [tinv[n] + _dot(tinv[n], npow[n]) for n in nch]

    akv = [_dot(ak[n], v_s[n]) for n in nch]
    p12 = [_dot(tinv[n], jnp.concatenate([a_s[n], akv[n]], axis=1)) for n in nch]
    pv = [jnp.concatenate([p12[n][:, LANES:], v_s[n]], axis=0) for n in nch]
    q1 = [r_s[n] + _dot(rbk[n][:, :LANES], p12[n][:, :LANES]) for n in nch]
    q2 = [_dot(rbk[n], pv[n]) for n in nch]
    m1 = [jnp.where(eye, jnp.exp(tot[n]), 0.0) + _dot(bh_s[n].T, p12[n][:, :LANES]) for n in nch]
    m2 = [_dot(jnp.concatenate([bh_s[n], kh_s[n]], axis=0).T, pv[n]) for n in nch]

    ys = []
    st = st_ref[...]
    for n in nch:
        both = _dot(jnp.concatenate([q1[n], m1[n]], axis=0), st)
        y_s = both[:LANES] + q2[n]
        st = both[LANES:] + m2[n]
        ys.append(y_s[:c] + y_s[c:])
    st_ref[...] = st

    y = jnp.concatenate(ys, axis=0)
    gmu = _dot32(y, head_ones) * (1.0 / hd)
    yc = y - gmu
    var = _dot32(yc * yc, head_ones) * (1.0 / hd)
    yn = yc * lax.rsqrt(var + RWKV_GN_EPS) * gnw_ref[...] + gnb_ref[...]
    o_ref[...] = ((yn + bonus) * g).astype(o_ref.dtype)


def rwkv7_mixer(feat, lora, mu, mu_lora, w0, a0, k_k, k_a, r_k, gn_w, gn_b, w2p, a2p, g2):
    bsz, s, _ = feat.shape
    tb = min(RWKV_TB, s)
    assert s % tb == 0 and tb % CHUNK == 0
    w = RWKV_WIDTH
    npair = w // LANES
    nlo = RWKV_LORA_COLS
    pr = tb // 8

    def cur(width, col):
        return pl.BlockSpec((None, tb, width), lambda b, p, t: (b, t, col(p)))

    def prv(width, col):
        return pl.BlockSpec((None, 8, width),
                            lambda b, p, t: (b, jnp.maximum(t * pr - 1, 0), col(p)))

    def par(arr_cols, col):
        return pl.BlockSpec((1, arr_cols), lambda b, p, t: (0, col(p)))

    cr, ck, cv = (lambda p: p), (lambda p: npair + p), (lambda p: 2 * npair + p)
    clo = lambda p: 0
    vec = lambda x: x.reshape(1, -1)
    in_specs = [
        cur(LANES, cr), cur(LANES, ck), cur(LANES, cv), cur(nlo, clo),
        prv(LANES, cr), prv(LANES, ck), prv(LANES, cv), prv(nlo, clo),
        par(LANES, cr), par(LANES, ck), par(LANES, cv), par(nlo, clo),
    ] + [par(LANES, cr)] * 7 + [
        pl.BlockSpec((LANES, LANES), lambda b, p, t: (0, p)),
        pl.BlockSpec((LANES, LANES), lambda b, p, t: (0, p)),
        pl.BlockSpec((LORA_G, LANES), lambda b, p, t: (0, p)),
    ]
    return pl.pallas_call(
        functools.partial(_rwkv_kernel, tb=tb),
        out_shape=jax.ShapeDtypeStruct((bsz, s, w), BF16),
        grid=(bsz, npair, s // tb),
        in_specs=in_specs,
        out_specs=pl.BlockSpec((None, tb, LANES), lambda b, p, t: (b, t, p)),
        scratch_shapes=[pltpu.VMEM((LANES, LANES), F32)],
        compiler_params=_cparams(("parallel", "parallel", "arbitrary")),
        name="rwkv7",
    )(feat, feat, feat, lora, feat, feat, feat, lora,
      vec(mu), vec(mu), vec(mu), vec(mu_lora),
      vec(w0), vec(a0), vec(k_k), vec(k_a), vec(r_k), vec(gn_w), vec(gn_b),
      w2p, a2p, g2)


def _pad_cols(x, n):
    return jnp.pad(x, ((0, 0), (0, n - x.shape[1])))


def _pad_rows(x, n):
    return jnp.pad(x, ((0, n - x.shape[0]), (0, 0)))


def _rwkv_lora_cols(x):
    c0 = 3 * RWKV_WIDTH
    cw = x[..., c0:c0 + LORA_W]
    ca = x[..., c0 + LORA_W:c0 + LORA_W + LORA_A]
    cg = x[..., c0 + LORA_W + LORA_A:]
    pad = lambda y: jnp.pad(y, [(0, 0)] * (y.ndim - 1) + [(0, LANES - y.shape[-1])])
    return jnp.concatenate([pad(cw), pad(ca), cg], axis=-1)


ROUTER_COLS = LANES


def _router_params(rg, rgb, re, reb):
    d = rg.shape[0]
    w = jnp.zeros((d, ROUTER_COLS), F32).at[:, :N_GROUPS].set(rg)
    w = w.at[:, N_GROUPS:N_GROUPS + N_EXPERTS].set(re)
    b = jnp.zeros((ROUTER_COLS,), F32).at[:N_GROUPS].set(rgb)
    b = b.at[N_GROUPS:N_GROUPS + N_EXPERTS].set(reb)
    return w, b


def _tail(h_f32, mix_parts, w_out, ln1_g, ln1_b, rg, rgb, re, reb, wg, wu, wd, ln2_g, ln2_b):
    mixed = proj(mix_parts, w_out, out_dtype=F32)
    rw, rb = _router_params(rg, rgb, re, reb)
    h1_f32, _, logits = ln_router(h_f32, mixed, ln1_g, ln1_b, rw, rb)
    return hier_moe_ln(h1_f32, logits, wg, wu, wd, ln2_g, ln2_b)


def kernel(x, l0_w_in, l0_shift_mu, l0_w0, l0_w2, l0_a0, l0_a2, l0_g2, l0_k_k, l0_k_a, l0_r_k, l0_gn_w, l0_gn_b, l0_w_out, l0_ln1_g, l0_ln1_b, l0_router_g, l0_router_g_b, l0_router_e, l0_router_e_b, l0_w_gate, l0_w_up, l0_w_down, l0_ln2_g, l0_ln2_b, l1_w_in, l1_lq1, l1_lk1, l1_lq2, l1_lk2, l1_subln_w, l1_gate_w2, l1_gate_b, l1_gla_norm_w, l1_w_out, l1_ln1_g, l1_ln1_b, l1_router_g, l1_router_g_b, l1_router_e, l1_router_e_b, l1_w_gate, l1_w_up, l1_w_down, l1_ln2_g, l1_ln2_b):
    bsz, s, d = x.shape
    t = bsz * s
    h0 = x.reshape(t, d)

    sbw = 3 * SB_WIDTH
    rww = 3 * RWKV_WIDTH
    x_bf16 = h0.astype(BF16)
    qkv0 = proj([x_bf16], l0_w_in, col0=0, ncols=sbw, out_dtype=BF16)
    feat = proj([x_bf16], l0_w_in, col0=sbw, ncols=rww, out_dtype=F32)
    lora = proj([x_bf16], _rwkv_lora_cols(l0_w_in[:, sbw:]), out_dtype=F32)
    o_sb = sb_attention(qkv0.reshape(bsz, s, sbw))
    o_rw = rwkv7_mixer(feat.reshape(bsz, s, rww), lora.reshape(bsz, s, RWKV_LORA_COLS),
                       l0_shift_mu[:rww], _rwkv_lora_cols(l0_shift_mu[None])[0],
                       l0_w0, l0_a0, l0_k_k, l0_k_a, l0_r_k.reshape(-1), l0_gn_w, l0_gn_b,
                       _pad_rows(l0_w2, LANES), _pad_rows(l0_a2, LANES), l0_g2)
    h2_f32, h2_bf16 = _tail(h0, [o_sb.reshape(t, SB_WIDTH), o_rw.reshape(t, RWKV_WIDTH)],
                            l0_w_out, l0_ln1_g, l0_ln1_b, l0_router_g, l0_router_g_b,
                            l0_router_e, l0_router_e_b, l0_w_gate, l0_w_up, l0_w_down,
                            l0_ln2_g, l0_ln2_b)

    dfw = 3 * DIFF_WIDTH
    glw = 2 * GLA_QK + GLA_WIDTH
    g0 = dfw + glw
    g1 = g0 + GLA_GATE_RANK
    qkv1 = proj([h2_bf16], l1_w_in, col0=0, ncols=dfw, out_dtype=BF16)
    gproj = proj([h2_bf16], l1_w_in, col0=dfw, ncols=glw, out_dtype=F32)
    og = proj([h2_bf16], l1_w_in[:, g1:], out_dtype=F32)
    cgp = proj_exact(h2_f32, _pad_cols(l1_w_in[:, g0:g1], LANES))
    lam_init = 0.8 - 0.6 * math.exp(-0.3 * 1)
    lam = (jnp.exp(jnp.sum(l1_lq1 * l1_lk1)) - jnp.exp(jnp.sum(l1_lq2 * l1_lk2)) + lam_init)
    o_diff = diff_attention(qkv1.reshape(bsz, s, dfw), lam, l1_subln_w, lam_init)
    o_gla = gla_mixer(gproj.reshape(bsz, s, glw), og.reshape(bsz, s, GLA_WIDTH),
                      cgp.reshape(bsz, s, LANES), _pad_rows(l1_gate_w2, LANES), l1_gate_b,
                      l1_gla_norm_w)
    out_f32, _ = _tail(h2_f32, [o_diff.reshape(t, DIFF_WIDTH), o_gla.reshape(t, GLA_WIDTH)],
                       l1_w_out, l1_ln1_g, l1_ln1_b, l1_router_g, l1_router_g_b,
                       l1_router_e, l1_router_e_b, l1_w_gate, l1_w_up, l1_w_down,
                       l1_ln2_g, l1_ln2_b)
    return out_f32.reshape(bsz, s, d)
```

```python
import functools
import math

import jax
import jax.numpy as jnp
from jax import lax
from jax.experimental import pallas as pl
from jax.experimental.pallas import tpu as pltpu

F32 = jnp.float32
BF16 = jnp.bfloat16
I32 = jnp.int32

D_MODEL = 4096
DEPTH = 2
CHUNK = 64
SB_WIDTH = 2048
SB_HEAD_DIM = 128
SB_HEADS = SB_WIDTH // SB_HEAD_DIM
RWKV_WIDTH = 2048
RWKV_HEAD_DIM = 64
LORA_W = 96
LORA_A = 96
LORA_G = 256
RWKV_DECAY_SCALE = math.exp(-0.5)
RWKV_GN_EPS = 64e-5
DIFF_WIDTH = 2048
DIFF_HEAD_DIM = 128
DIFF_HEADS = DIFF_WIDTH // (2 * DIFF_HEAD_DIM)
GLA_WIDTH = 2048
GLA_HEADS = 4
GLA_QK = GLA_WIDTH // 2
GLA_DK = GLA_QK // GLA_HEADS
GLA_DV = GLA_WIDTH // GLA_HEADS
GLA_GATE_RANK = 16
GLA_GATE_NORM = 16.0
N_GROUPS = 4
EXPERTS_PER_GROUP = 8
N_EXPERTS = N_GROUPS * EXPERTS_PER_GROUP
TOP_K = 2
D_EXPERT = 768
DEEPNORM_ALPHA = (2 * DEPTH) ** 0.25
LN_EPS = 1e-5

LANES = 128
V7X_VMEM_LIMIT = 56 * 1024 * 1024

def _cparams(semantics, vmem=V7X_VMEM_LIMIT):
    return pltpu.CompilerParams(dimension_semantics=semantics, vmem_limit_bytes=vmem)


def _dot(a, b, dims=(((1,), (0,)), ((), ()))):
    return lax.dot_general(a.astype(BF16), b.astype(BF16), dims, preferred_element_type=F32)


def _split2(x):
    hi = x.astype(BF16)
    return hi, (x - hi.astype(F32)).astype(BF16)


def _dot_split(a, b, *, exact, dims=(((1,), (0,)), ((), ()))):
    dn = lambda x, y: lax.dot_general(x, y, dims, preferred_element_type=F32)
    if exact == 'a':
        hi, lo = _split2(b)
        a16 = a.astype(BF16)
        return dn(a16, hi) + dn(a16, lo)
    hi, lo = _split2(a)
    b16 = b.astype(BF16)
    return dn(hi, b16) + dn(lo, b16)


def _dot3(a, b):
    ah, al = _split2(a)
    bh, bl = _split2(b)
    dn = lambda x, y: lax.dot_general(x, y, (((1,), (0,)), ((), ())), preferred_element_type=F32)
    return dn(ah, bh) + dn(ah, bl) + dn(al, bh)


_NT = (((1,), (1,)), ((), ()))
_TN = (((0,), (0,)), ((), ()))


PROJ_TM = 1024
PROJ_TN = 512


def _proj_kernel(*refs, nx):
    x_refs, w_ref, o_ref, wb_ref = refs[:nx], refs[nx], refs[nx + 1], refs[nx + 2]

    @pl.when(pl.program_id(1) == 0)
    def _():
        wb_ref[...] = w_ref[...].astype(BF16)

    acc = None
    k0 = 0
    for x_ref in x_refs:
        kx = x_ref.shape[1]
        part = lax.dot_general(x_ref[...], wb_ref[k0:k0 + kx, :], (((1,), (0,)), ((), ())),
                               preferred_element_type=F32)
        acc = part if acc is None else acc + part
        k0 += kx
    o_ref[...] = acc.astype(o_ref.dtype)


def proj(xs, w, *, col0=0, ncols=None, out_dtype):
    m = xs[0].shape[0]
    k = w.shape[0]
    ncols = w.shape[1] - col0 if ncols is None else ncols
    tm, tn = PROJ_TM, PROJ_TN
    assert sum(x.shape[1] for x in xs) == k and m % tm == 0
    assert ncols % tn == 0 and col0 % tn == 0, (col0, ncols)
    cb = col0 // tn
    in_specs = [pl.BlockSpec((tm, x.shape[1]), lambda j, i: (i, 0)) for x in xs]
    in_specs.append(pl.BlockSpec((k, tn), lambda j, i: (0, cb + j)))
    return pl.pallas_call(
        functools.partial(_proj_kernel, nx=len(xs)),
        out_shape=jax.ShapeDtypeStruct((m, ncols), out_dtype),
        grid=(ncols // tn, m // tm),
        in_specs=in_specs,
        out_specs=pl.BlockSpec((tm, tn), lambda j, i: (i, j)),
        scratch_shapes=[pltpu.VMEM((k, tn), BF16)],
        compiler_params=_cparams(("arbitrary", "arbitrary")),
        name="proj",
    )(*xs, w)


def _proj_exact_kernel(x_ref, w_ref, o_ref):
    o_ref[...] = _dot3(x_ref[...], w_ref[...])


def proj_exact(x, w, *, tm=512):
    m, k = x.shape
    n = w.shape[1]
    return pl.pallas_call(
        _proj_exact_kernel,
        out_shape=jax.ShapeDtypeStruct((m, n), F32),
        grid=(m // tm,),
        in_specs=[pl.BlockSpec((tm, k), lambda i: (i, 0)), pl.BlockSpec((k, n), lambda i: (0, 0))],
        out_specs=pl.BlockSpec((tm, n), lambda i: (i, 0)),
        compiler_params=_cparams(("parallel",)),
        name="proj_exact",
    )(x, w)


def _layer_norm_rows(x, g, b):
    mu = jnp.mean(x, axis=-1, keepdims=True)
    xc = x - mu
    var = jnp.mean(xc * xc, axis=-1, keepdims=True)
    return xc * lax.rsqrt(var + LN_EPS) * g + b


def _ln_router_kernel(h_ref, m_ref, g_ref, b_ref, r_ref, rb_ref, of_ref, ob_ref, lg_ref):
    y = _layer_norm_rows(DEEPNORM_ALPHA * h_ref[...] + m_ref[...], g_ref[...], b_ref[...])
    of_ref[...] = y
    ob_ref[...] = y.astype(BF16)
    lg_ref[...] = _dot3(y, r_ref[...]) + rb_ref[...]


def ln_router(h, mixed, g, b, router_w, router_b, *, tm=256):
    t, d = h.shape
    nr = router_w.shape[1]
    row = lambda i: (i, 0)
    fixed = lambda i: (0, 0)
    return pl.pallas_call(
        _ln_router_kernel,
        out_shape=(jax.ShapeDtypeStruct((t, d), F32), jax.ShapeDtypeStruct((t, d), BF16),
                   jax.ShapeDtypeStruct((t, nr), F32)),
        grid=(t // tm,),
        in_specs=[pl.BlockSpec((tm, d), row), pl.BlockSpec((tm, d), row),
                  pl.BlockSpec((1, d), fixed), pl.BlockSpec((1, d), fixed),
                  pl.BlockSpec((d, nr), fixed), pl.BlockSpec((1, nr), fixed)],
        out_specs=(pl.BlockSpec((tm, d), row), pl.BlockSpec((tm, d), row),
                   pl.BlockSpec((tm, nr), row)),
        compiler_params=_cparams(("parallel",)),
        name="ln_router",
    )(h, mixed, g.reshape(1, d), b.reshape(1, d), router_w, router_b.reshape(1, nr))


MOE_ROWS = 256
MOE_UP_SPLIT = 2
MOE_TM = 256
U32 = jnp.uint32


def _pack_halves(y):
    n = y.shape[1] // 2
    bits = pltpu.bitcast(y, U32)
    rounded = bits + U32(0x7FFF) + ((bits >> 16) & U32(1))
    return (rounded[:, :n] & U32(0xFFFF0000)) | (rounded[:, n:] >> 16)


def _unpack_halves(u):
    hi = pltpu.bitcast(u & U32(0xFFFF0000), F32).astype(BF16)
    lo = pltpu.bitcast(u << 16, F32).astype(BF16)
    return hi, lo


DMA_UNROLL = 8

ET_NEXT, ET_SLOT, ET_LAST, ET_HAS, ET_NB = 0, 1, 2, 3, 4


def _dispatch_kernel(pos_ref, et_ref, h_ref, xs_out, buf, zbuf, sem, zsem, *, tm, nsteps):
    i = pl.program_id(0)
    slot = i % 2

    @pl.when(i == 0)
    def _():
        zbuf[...] = jnp.zeros(zbuf.shape, zbuf.dtype)

        def zero_copy(e):
            first = pl.multiple_of(et_ref[ET_LAST, e], MOE_ROWS)
            return pltpu.make_async_copy(zbuf, xs_out.at[pl.ds(first, MOE_ROWS), :], zsem)

        def unused_copy(blk):
            first = pl.multiple_of(blk * MOE_ROWS, MOE_ROWS)
            return pltpu.make_async_copy(zbuf, xs_out.at[pl.ds(first, MOE_ROWS), :], zsem)

        nblk = xs_out.shape[0] // MOE_ROWS
        nb = et_ref[ET_NB, 0]
        for e in range(N_EXPERTS):
            @pl.when(et_ref[ET_HAS, e] > 0)
            def _():
                zero_copy(e).start()
        lax.fori_loop(nb, nblk, lambda blk, c: (unused_copy(blk).start(), c)[1], 0)
        for e in range(N_EXPERTS):
            @pl.when(et_ref[ET_HAS, e] > 0)
            def _():
                zero_copy(e).wait()
        lax.fori_loop(nb, nblk, lambda blk, c: (unused_copy(blk).wait(), c)[1], 0)

    def row_copy(s, r, p):
        return pltpu.make_async_copy(buf.at[s, pl.ds(r, 1), :], xs_out.at[pl.ds(p, 1), :],
                                     sem.at[s])

    def drain(s):
        def body(r, c):
            for _ in range(TOP_K):
                row_copy(s, r, 0).wait()
            return c
        lax.fori_loop(0, tm, body, 0, unroll=DMA_UNROLL)

    @pl.when(i >= 2)
    def _():
        drain(slot)

    buf[slot] = _pack_halves(h_ref[...])

    def issue(r, c):
        a = (i * tm + r) * TOP_K
        for k in range(TOP_K):
            row_copy(slot, r, pos_ref[a + k]).start()
        return c
    lax.fori_loop(0, tm, issue, 0, unroll=DMA_UNROLL)

    @pl.when(i == nsteps - 1)
    def _():
        drain(slot)
        if nsteps > 1:
            drain(1 - slot)


def moe_dispatch(h, pos, etab, nslots):
    t, d = h.shape
    tm = MOE_TM
    nsteps = t // tm
    grid_spec = pltpu.PrefetchScalarGridSpec(
        num_scalar_prefetch=2,
        grid=(nsteps,),
        in_specs=[pl.BlockSpec((tm, d), lambda i, pos, et: (i, 0))],
        out_specs=pl.BlockSpec(memory_space=pl.ANY),
        scratch_shapes=[pltpu.VMEM((2, tm, d // 2), U32), pltpu.VMEM((MOE_ROWS, d // 2), U32),
                        pltpu.SemaphoreType.DMA((2,)), pltpu.SemaphoreType.DMA(())],
    )
    return pl.pallas_call(
        functools.partial(_dispatch_kernel, tm=tm, nsteps=nsteps),
        out_shape=jax.ShapeDtypeStruct((nslots, d // 2), U32),
        grid_spec=grid_spec,
        compiler_params=_cparams(("arbitrary",)),
        name="moe_dispatch",
    )(pos.reshape(-1), etab, h)


def _expert_changed(be_ref, i):
    return (i == 0) | (be_ref[i] != be_ref[jnp.maximum(i - 1, 0)])


def _expert_weights(be_ref, et_ref, i, copies, convert):
    e = be_ref[i]
    slot = et_ref[ET_SLOT, e]

    @pl.when(_expert_changed(be_ref, i))
    def _():
        @pl.when(i == 0)
        def _():
            for c in copies(e, slot):
                c.start()
        for c in copies(e, slot):
            c.wait()
        nxt = et_ref[ET_NEXT, e]

        @pl.when(nxt >= 0)
        def _():
            for c in copies(nxt, 1 - slot):
                c.start()
        convert(slot)


def _moe_up_kernel(be_ref, nb_ref, et_ref, xs_ref, wg_hbm, wu_hbm, a_ref, wst, wc_ref, sem):
    j = pl.program_id(0)
    i = pl.program_id(1)
    half = xs_ref.shape[1]
    ce = a_ref.shape[1]

    def copies(e, slot):
        cols = pl.ds(pl.multiple_of(j * ce, LANES), ce)
        return [pltpu.make_async_copy(wg_hbm.at[e, :, cols], wst.at[slot, 0], sem.at[slot]),
                pltpu.make_async_copy(wu_hbm.at[e, :, cols], wst.at[slot, 1], sem.at[slot])]

    def convert(slot):
        for hh in range(2):
            rows = slice(hh * half, (hh + 1) * half)
            wc_ref[hh] = jnp.concatenate([wst[slot, 0, rows, :], wst[slot, 1, rows, :]],
                                         axis=1).astype(BF16)

    @pl.when(i < nb_ref[0])
    def _():
        _expert_weights(be_ref, et_ref, i, copies, convert)
        hi, lo = _unpack_halves(xs_ref[...])
        hgu = _dot(hi, wc_ref[0]) + _dot(lo, wc_ref[1])
        hg, hu = hgu[:, :ce], hgu[:, ce:]
        a_ref[...] = (hg * jax.nn.sigmoid(hg) * hu).astype(a_ref.dtype)

    @pl.when(i >= nb_ref[0])
    def _():
        a_ref[...] = jnp.zeros_like(a_ref)


def moe_up(xs, block_e, nb_used, etab, wg, wu):
    nslots, half = xs.shape
    _, d, de = wg.shape
    rows = MOE_ROWS
    nblk = nslots // rows
    ce = de // MOE_UP_SPLIT
    last = lambda i, nb: jnp.minimum(i, nb[0] - 1)
    grid_spec = pltpu.PrefetchScalarGridSpec(
        num_scalar_prefetch=3,
        grid=(MOE_UP_SPLIT, nblk),
        in_specs=[
            pl.BlockSpec((rows, half), lambda j, i, be, nb, et: (last(i, nb), 0)),
            pl.BlockSpec(memory_space=pl.ANY),
            pl.BlockSpec(memory_space=pl.ANY),
        ],
        out_specs=pl.BlockSpec((rows, ce), lambda j, i, be, nb, et: (i, j)),
        scratch_shapes=[pltpu.VMEM((2, 2, d, ce), F32), pltpu.VMEM((2, half, 2 * ce), BF16),
                        pltpu.SemaphoreType.DMA((2,))],
    )
    return pl.pallas_call(
        _moe_up_kernel,
        out_shape=jax.ShapeDtypeStruct((nslots, de), BF16),
        grid_spec=grid_spec,
        compiler_params=_cparams(("arbitrary", "arbitrary")),
        name="moe_up",
    )(block_e, nb_used, etab, xs, wg, wu)


def _moe_down_kernel(be_ref, nb_ref, et_ref, a_ref, wd_hbm, o_ref, wst, wc_ref, sem):
    i = pl.program_id(0)

    def copies(e, slot):
        return [pltpu.make_async_copy(wd_hbm.at[e], wst.at[slot], sem.at[slot])]

    def convert(slot):
        wc_ref[...] = wst[slot].astype(BF16)

    @pl.when(i < nb_ref[0])
    def _():
        _expert_weights(be_ref, et_ref, i, copies, convert)
        o_ref[...] = _dot(a_ref[...], wc_ref[...])

    @pl.when(i >= nb_ref[0])
    def _():
        o_ref[...] = jnp.zeros_like(o_ref)


def moe_down(act, block_e, nb_used, etab, wd):
    nslots, de = act.shape
    d = wd.shape[2]
    rows = MOE_ROWS
    nblk = nslots // rows
    last = lambda i, nb: jnp.minimum(i, nb[0] - 1)
    grid_spec = pltpu.PrefetchScalarGridSpec(
        num_scalar_prefetch=3,
        grid=(nblk,),
        in_specs=[
            pl.BlockSpec((rows, de), lambda i, be, nb, et: (last(i, nb), 0)),
            pl.BlockSpec(memory_space=pl.ANY),
        ],
        out_specs=pl.BlockSpec((rows, d), lambda i, be, nb, et: (i, 0)),
        scratch_shapes=[pltpu.VMEM((2, de, d), F32), pltpu.VMEM((de, d), BF16),
                        pltpu.SemaphoreType.DMA((2,))],
    )
    return pl.pallas_call(
        _moe_down_kernel,
        out_shape=jax.ShapeDtypeStruct((nslots, d), F32),
        grid_spec=grid_spec,
        compiler_params=_cparams(("arbitrary",)),
        name="moe_down",
    )(block_e, nb_used, etab, act, wd)


def _combine_ln_kernel(pos_ref, h_ref, gate_ref, yb_hbm, g_ref, b_ref, of_ref, ob_ref,
                       ybuf, sem, *, tm, nsteps):
    i = pl.program_id(0)
    slot = i % 2

    def row_copy(s, k, r, p):
        return pltpu.make_async_copy(yb_hbm.at[pl.ds(p, 1), :], ybuf.at[s, k, pl.ds(r, 1), :],
                                     sem.at[s])

    def issue(step, s):
        def body(r, c):
            a = (step * tm + r) * TOP_K
            for k in range(TOP_K):
                row_copy(s, k, r, pos_ref[a + k]).start()
            return c
        lax.fori_loop(0, tm, body, 0, unroll=DMA_UNROLL)

    @pl.when(i == 0)
    def _():
        issue(0, 0)

    @pl.when(i + 1 < nsteps)
    def _():
        issue(i + 1, 1 - slot)

    def drain(r, c):
        for k in range(TOP_K):
            row_copy(slot, k, r, 0).wait()
        return c
    lax.fori_loop(0, tm, drain, 0, unroll=DMA_UNROLL)

    gate = gate_ref[...]
    y = ybuf[slot, 0] * gate[:, 0:1]
    for k in range(1, TOP_K):
        y = y + ybuf[slot, k] * gate[:, k:k + 1]
    out = _layer_norm_rows(DEEPNORM_ALPHA * h_ref[...] + y, g_ref[...], b_ref[...])
    of_ref[...] = out
    ob_ref[...] = out.astype(BF16)


def combine_ln(h, gates, pos, yb, g, b):
    t, d = h.shape
    tm = MOE_TM
    nsteps = t // tm
    grid_spec = pltpu.PrefetchScalarGridSpec(
        num_scalar_prefetch=1,
        grid=(nsteps,),
        in_specs=[
            pl.BlockSpec((tm, d), lambda i, pos: (i, 0)),
            pl.BlockSpec((tm, TOP_K), lambda i, pos: (i, 0)),
            pl.BlockSpec(memory_space=pl.ANY),
            pl.BlockSpec((1, d), lambda i, pos: (0, 0)),
            pl.BlockSpec((1, d), lambda i, pos: (0, 0)),
        ],
        out_specs=(pl.BlockSpec((tm, d), lambda i, pos: (i, 0)),
                   pl.BlockSpec((tm, d), lambda i, pos: (i, 0))),
        scratch_shapes=[pltpu.VMEM((2, TOP_K, tm, d), F32), pltpu.SemaphoreType.DMA((2,))],
    )
    return pl.pallas_call(
        functools.partial(_combine_ln_kernel, tm=tm, nsteps=nsteps),
        out_shape=(jax.ShapeDtypeStruct((t, d), F32), jax.ShapeDtypeStruct((t, d), BF16)),
        grid_spec=grid_spec,
        compiler_params=_cparams(("arbitrary",)),
        name="combine_ln",
    )(pos.reshape(-1), h, gates, yb, g.reshape(1, d), b.reshape(1, d))


def _routing_tables(logits, t):
    g_logits = logits[:, :N_GROUPS]
    g_prob = jax.nn.softmax(g_logits, axis=-1)
    g_sel = jnp.argmax(g_logits, axis=-1)
    g_w = jnp.take_along_axis(g_prob, g_sel[:, None], axis=-1)[:, 0]
    e_logits = logits[:, N_GROUPS:N_GROUPS + N_EXPERTS].reshape(t, N_GROUPS, EXPERTS_PER_GROUP)
    e_in_group = jnp.take_along_axis(e_logits, g_sel[:, None, None], axis=1)[:, 0]
    top_vals, top_idx = lax.top_k(e_in_group, TOP_K)
    gates = jax.nn.softmax(top_vals, axis=-1) * g_w[:, None]
    expert_id = (g_sel[:, None] * EXPERTS_PER_GROUP + top_idx).astype(I32)

    a = t * TOP_K
    rows = MOE_ROWS
    flat_e = expert_id.reshape(a)
    onehot = (flat_e[:, None] == jnp.arange(N_EXPERTS, dtype=I32)[None, :]).astype(I32)
    seen = jnp.cumsum(onehot, axis=0)
    counts = seen[-1]
    padded = (counts + rows - 1) // rows * rows
    pad_end = jnp.cumsum(padded)
    pad_start = pad_end - padded
    pos = jnp.sum(onehot * (seen - 1 + pad_start[None, :]), axis=1).astype(I32)
    nblk = a // rows + N_EXPERTS
    block_start = jnp.arange(nblk, dtype=I32) * rows
    block_e = jnp.minimum(jnp.sum((pad_end[None, :] <= block_start[:, None]).astype(I32), axis=1),
                          N_EXPERTS - 1).astype(I32)
    nb_used = (pad_end[-1] // rows).astype(I32).reshape(1)
    eidx = jnp.arange(N_EXPERTS, dtype=I32)
    has = counts > 0
    from_here = lax.cummin(jnp.where(has, eidx, N_EXPERTS), reverse=True)
    nxt = jnp.concatenate([from_here[1:], jnp.full((1,), N_EXPERTS, I32)])
    nxt = jnp.where(nxt >= N_EXPERTS, -1, nxt)
    run_slot = (jnp.cumsum(has.astype(I32)) - has.astype(I32)) % 2
    last_first = jnp.where(has, pad_end - rows, 0)
    etab = jnp.stack([nxt, run_slot, last_first, has.astype(I32),
                      jnp.broadcast_to(nb_used, (N_EXPERTS,))]).astype(I32)
    return gates.astype(F32), pos.reshape(t, TOP_K), block_e, nb_used, etab, nblk * rows


def hier_moe_ln(h_f32, logits, wg, wu, wd, ln_g, ln_b):
    t = h_f32.shape[0]
    gates, pos, block_e, nb_used, etab, nslots = _routing_tables(logits, t)
    xs = moe_dispatch(h_f32, pos, etab, nslots)
    act = moe_up(xs, block_e, nb_used, etab, wg, wu)
    yb = moe_down(act, block_e, nb_used, etab, wd)
    return combine_ln(h_f32, gates, pos, yb, ln_g, ln_b)


DIFF_T = 512
NEG_BIG = -1e30


def _lane_tile(x, n):
    return x if n == 1 else jnp.concatenate([x] * n, axis=1)


DIFF_HEADS_PER_STEP = 2


def _diff_attn_kernel(lam_ref, q_ref, k_ref, v_ref, sw_ref, o_ref,
                      m_ref, l_ref, acc_ref, *, t, nhs, scale, out_scale):
    i = pl.program_id(2)
    d = DIFF_HEAD_DIM
    chains = range(2 * nhs)
    qk_cols = [slice(c * d, (c + 1) * d) for c in chains]
    v_cols = [slice(c // 2 * 2 * d, (c // 2 + 1) * 2 * d) for c in chains]
    qs = [(q_ref[:, qk_cols[c]].astype(F32) * scale).astype(BF16) for c in chains]
    m_ref[...] = jnp.full(m_ref.shape, NEG_BIG, F32)
    l_ref[...] = jnp.zeros(l_ref.shape, F32)
    acc_ref[...] = jnp.zeros(acc_ref.shape, F32)
    krep = t // LANES
    vrep = 2 * d // LANES

    def block(j, diagonal):
        start = pl.multiple_of(j * t, t)
        s = [lax.dot_general(qs[c], k_ref[pl.ds(start, t), qk_cols[c]], _NT,
                             preferred_element_type=F32) for c in chains]
        if diagonal:
            row_chunk = lax.broadcasted_iota(I32, (t, t), 0) // CHUNK
            col_chunk = lax.broadcasted_iota(I32, (t, t), 1) // CHUNK
            s = [jnp.where(col_chunk <= row_chunk, s[c], NEG_BIG) for c in chains]
        m_prev = [m_ref[c] for c in chains]
        m_new = [jnp.maximum(m_prev[c], jnp.max(s[c], axis=1, keepdims=True)) for c in chains]
        p = [jnp.exp(s[c] - _lane_tile(m_new[c], krep)) for c in chains]
        alpha = [jnp.exp(m_prev[c] - m_new[c]) for c in chains]
        l_new = [alpha[c] * l_ref[c] + jnp.sum(p[c], axis=1, keepdims=True) for c in chains]
        pv = [_dot(p[c], v_ref[pl.ds(start, t), v_cols[c]]) for c in chains]
        for c in chains:
            acc_ref[c] = _lane_tile(alpha[c], vrep) * acc_ref[c] + pv[c]
            m_ref[c] = m_new[c]
            l_ref[c] = l_new[c]

    def full_block(j, carry):
        block(j, False)
        return carry

    lax.fori_loop(0, i, full_block, 0)
    block(i, True)
    lam = lam_ref[0]
    for h in range(nhs):
        inv = [_lane_tile(1.0 / l_ref[2 * h + b], vrep) for b in range(2)]
        o = acc_ref[2 * h] * inv[0] - lam * (acc_ref[2 * h + 1] * inv[1])
        ms = jnp.mean(o * o, axis=-1, keepdims=True)
        o_ref[:, v_cols[2 * h]] = (o * lax.rsqrt(ms + 1e-5) * sw_ref[...]
                                   * out_scale).astype(o_ref.dtype)


def diff_attention(qkv, lam, subln_w, lam_init):
    bsz, s, _ = qkv.shape
    tq = min(DIFF_T, s)
    assert s % tq == 0 and tq % CHUNK == 0
    d = DIFF_HEAD_DIM
    nhs = DIFF_HEADS_PER_STEP
    w = nhs * 2 * d
    ng = DIFF_WIDTH // w
    grid_spec = pltpu.PrefetchScalarGridSpec(
        num_scalar_prefetch=0,
        grid=(bsz, ng, s // tq),
        in_specs=[
            pl.BlockSpec(memory_space=pltpu.SMEM),
            pl.BlockSpec((None, tq, w), lambda b, g, i: (b, i, g)),
            pl.BlockSpec((None, s, w), lambda b, g, i: (b, 0, ng + g)),
            pl.BlockSpec((None, s, w), lambda b, g, i: (b, 0, 2 * ng + g)),
            pl.BlockSpec((1, 2 * d), lambda b, g, i: (0, 0)),
        ],
        out_specs=pl.BlockSpec((None, tq, w), lambda b, g, i: (b, i, g)),
        scratch_shapes=[pltpu.VMEM((2 * nhs, tq, LANES), F32),
                        pltpu.VMEM((2 * nhs, tq, LANES), F32),
                        pltpu.VMEM((2 * nhs, tq, 2 * d), F32)],
    )
    return pl.pallas_call(
        functools.partial(_diff_attn_kernel, t=tq, nhs=nhs, scale=d ** -0.5,
                          out_scale=1.0 - lam_init),
        out_shape=jax.ShapeDtypeStruct((bsz, s, DIFF_WIDTH), BF16),
        grid_spec=grid_spec,
        compiler_params=_cparams(("parallel", "parallel", "arbitrary")),
        name="diff_attn",
    )(lam.reshape(1).astype(F32), qkv, qkv, qkv, subln_w.reshape(1, 2 * d))


GLA_TB = 512


def _log_sigmoid(x):
    return jnp.minimum(x, 0.0) - jnp.log(1.0 + jnp.exp(-jnp.abs(x)))


def _gla_kernel(q_ref, k_ref, v_ref, og_ref, cg_ref, gw_ref, gb_ref, nw_ref, o_ref,
                state_ref, *, tb, scale):
    t = pl.program_id(2)

    @pl.when(t == 0)
    def _():
        state_ref[...] = jnp.zeros(state_ref.shape, F32)

    c = CHUNK
    ri = lax.broadcasted_iota(I32, (c, c), 0)
    ci = lax.broadcasted_iota(I32, (c, c), 1)
    causal = ci <= ri
    ltri = causal.astype(F32)
    ones_cols = jnp.ones((c, LANES), F32)
    dv = v_ref.shape[-1]
    nch = range(tb // c)
    rows = [slice(n * c, (n + 1) * c) for n in nch]
    gl = [_dot3(cg_ref[rows[n], :], gw_ref[...]) + gb_ref[...] for n in nch]
    log_a = [_log_sigmoid(gl[n]) * (1.0 / GLA_GATE_NORM) for n in nch]
    bcum = [_dot_split(ltri, log_a[n], exact='a') for n in nch]
    v = [v_ref[rows[n], :] for n in nch]
    q_dec = [q_ref[rows[n], :] * scale * jnp.exp(bcum[n]) for n in nch]
    k_inv = [k_ref[rows[n], :] * jnp.exp(-bcum[n]) for n in nch]
    k_end = [k_ref[rows[n], :] * jnp.exp(bcum[n][c - 1:c, :] - bcum[n]) for n in nch]
    scores = [jnp.where(causal, _dot(q_dec[n], k_inv[n], _NT), 0.0) for n in nch]
    o_intra = [_dot(scores[n], v[n]) for n in nch]
    dec = [_lane_tile(jnp.exp(_dot_split(log_a[n], ones_cols, exact='b', dims=_TN)),
                      dv // LANES) for n in nch]
    kv = [_dot(k_end[n].T, v[n]) for n in nch]

    state = state_ref[...]
    o = []
    for n in nch:
        o.append(o_intra[n] + _dot(q_dec[n], state))
        state = state * dec[n] + kv[n]
    state_ref[...] = state

    for n in nch:
        ms = jnp.mean(o[n] * o[n], axis=-1, keepdims=True)
        og = og_ref[rows[n], :]
        gate = og * jax.nn.sigmoid(og)
        o_ref[rows[n], :] = (o[n] * lax.rsqrt(ms + 1e-5) * nw_ref[...] * gate).astype(o_ref.dtype)


def gla_mixer(proj, og, cg, gate_w2p, gate_b, norm_w):
    bsz, s, _ = proj.shape
    tb = min(GLA_TB, s)
    assert s % tb == 0 and tb % CHUNK == 0
    dk, dv, nh = GLA_DK, GLA_DV, GLA_HEADS
    in_specs = [
        pl.BlockSpec((None, tb, dk), lambda b, h, t: (b, t, h)),
        pl.BlockSpec((None, tb, dk), lambda b, h, t: (b, t, nh + h)),
        pl.BlockSpec((None, tb, dv), lambda b, h, t: (b, t, nh + h)),
        pl.BlockSpec((None, tb, dv), lambda b, h, t: (b, t, h)),
        pl.BlockSpec((None, tb, LANES), lambda b, h, t: (b, t, 0)),
        pl.BlockSpec((LANES, dk), lambda b, h, t: (0, h)),
        pl.BlockSpec((1, dk), lambda b, h, t: (0, h)),
        pl.BlockSpec((1, dv), lambda b, h, t: (0, 0)),
    ]
    return pl.pallas_call(
        functools.partial(_gla_kernel, tb=tb, scale=dk ** -0.5),
        out_shape=jax.ShapeDtypeStruct((bsz, s, GLA_WIDTH), BF16),
        grid=(bsz, nh, s // tb),
        in_specs=in_specs,
        out_specs=pl.BlockSpec((None, tb, dv), lambda b, h, t: (b, t, h)),
        scratch_shapes=[pltpu.VMEM((dk, dv), F32)],
        compiler_params=_cparams(("parallel", "parallel", "arbitrary")),
        name="gla",
    )(proj, proj, proj, og, cg, gate_w2p, gate_b.reshape(1, GLA_QK), norm_w.reshape(1, dv))


SB_T = 256
SB_HEADS_PER_STEP = 2
SB_UNDERFLOW = 110.0


def _sb_attn_kernel(q_ref, k_ref, v_ref, u_ref, o_ref, acc_ref, run_ref, *, t, nhs, scale):
    i = pl.program_id(2)
    d = SB_HEAD_DIM
    heads = range(nhs)
    cols = [slice(h * d, (h + 1) * d) for h in heads]
    q = [(q_ref[:, cols[h]].astype(F32) * scale).astype(BF16) for h in heads]
    u = u_ref[...]
    rep = t // LANES

    def block(j, diagonal, first):
        start = pl.multiple_of(j * t, t)
        kb = [k_ref[pl.ds(start, t), cols[h]] for h in heads]
        vb = [v_ref[pl.ds(start, t), cols[h]] for h in heads]
        z = [lax.dot_general(q[h], kb[h], _NT, preferred_element_type=F32) for h in heads]
        sp = [jnp.maximum(z[h], 0.0) + jnp.log(1.0 + jnp.exp(-jnp.abs(z[h]))) for h in heads]
        if diagonal:
            strict = (lax.broadcasted_iota(I32, (t, t), 1) < lax.broadcasted_iota(I32, (t, t), 0))
            sp = [jnp.where(strict, sp[h], 0.0) for h in heads]
        hi = [sp[h].astype(BF16) for h in heads]
        lo = [(sp[h] - hi[h].astype(F32)).astype(BF16) for h in heads]
        sums = [lax.dot_general(hi[h], u, (((1,), (0,)), ((), ())), preferred_element_type=F32)
                + lax.dot_general(lo[h], u, (((1,), (0,)), ((), ())), preferred_element_type=F32)
                for h in heads]
        if first:
            x = [z[h] + sums[h] for h in heads]
        else:
            x = [z[h] + sums[h] + _lane_tile(run_ref[h], rep) for h in heads]
        w = [jnp.exp(x[h]) for h in heads]
        if diagonal:
            w = [jnp.where(strict, w[h], 0.0) for h in heads]
        pv = [_dot(w[h], vb[h]) for h in heads]
        total = [jnp.broadcast_to(sums[h][:, 0:1], (t, LANES)) for h in heads]
        run_max = None
        for h in heads:
            if first:
                acc_ref[:, cols[h]] = pv[h]
                run_new = total[h]
            else:
                acc_ref[:, cols[h]] += pv[h]
                run_new = run_ref[h] + total[h]
            run_ref[h] = run_new
            m = jnp.max(run_new)
            run_max = m if run_max is None else jnp.maximum(run_max, m)
        return (run_max > -SB_UNDERFLOW).astype(I32)

    alive0 = block(i, True, True)

    def cond(c):
        return (c[0] <= i) & (c[1] > 0)

    def body(c):
        return c[0] + 1, block(i - c[0], False, False)

    lax.while_loop(cond, body, (jnp.int32(1), alive0))
    o_ref[...] = acc_ref[...].astype(o_ref.dtype)


def sb_attention(qkv):
    bsz, s, _ = qkv.shape
    t = min(SB_T, s)
    assert s % t == 0
    nhs = SB_HEADS_PER_STEP
    w = nhs * SB_HEAD_DIM
    ng = SB_HEADS // nhs
    r = lax.broadcasted_iota(I32, (t, t), 0)
    c = lax.broadcasted_iota(I32, (t, t), 1)
    u = jnp.where(r >= c, -1.0, 0.0).astype(BF16)
    return pl.pallas_call(
        functools.partial(_sb_attn_kernel, t=t, nhs=nhs, scale=SB_HEAD_DIM ** -0.5),
        out_shape=jax.ShapeDtypeStruct((bsz, s, SB_WIDTH), BF16),
        grid=(bsz, ng, s // t),
        in_specs=[
            pl.BlockSpec((None, t, w), lambda b, g, i: (b, i, g)),
            pl.BlockSpec((None, s, w), lambda b, g, i: (b, 0, ng + g)),
            pl.BlockSpec((None, s, w), lambda b, g, i: (b, 0, 2 * ng + g)),
            pl.BlockSpec((t, t), lambda b, g, i: (0, 0)),
        ],
        out_specs=pl.BlockSpec((None, t, w), lambda b, g, i: (b, i, g)),
        scratch_shapes=[pltpu.VMEM((t, w), F32), pltpu.VMEM((nhs, t, LANES), F32)],
        compiler_params=_cparams(("parallel", "parallel", "arbitrary")),
        name="sb_attn",
    )(qkv, qkv, qkv, u)


RWKV_TB = 2048
RWKV_GROUP = 8
RWKV_PAIR = LANES // RWKV_HEAD_DIM
RWKV_LORA_COLS = 512


def _rwkv_kernel(r_ref, k_ref, v_ref, lo_ref, rp_ref, kp_ref, vp_ref, lop_ref,
                 mur_ref, muk_ref, muv_ref, mulo_ref,
                 w0_ref, a0_ref, kk_ref, ka_ref, rk_ref, gnw_ref, gnb_ref,
                 w2_ref, a2_ref, g2_ref, o_ref, st_ref, *, tb):
    t = pl.program_id(2)
    c = CHUNK
    hd = RWKV_HEAD_DIM
    nch = range(tb // c)

    @pl.when(t == 0)
    def _():
        st_ref[...] = jnp.zeros(st_ref.shape, F32)

    li = lax.broadcasted_iota(I32, (LANES, LANES), 0)
    lj = lax.broadcasted_iota(I32, (LANES, LANES), 1)
    head_ones = (li // hd == lj // hd).astype(F32)

    def shifted(x_ref, p_ref, mu_ref):
        x = x_ref[...]
        carry = jnp.where(t == 0, 0.0, p_ref[7:8, :])
        row = lax.broadcasted_iota(I32, x.shape, 0)
        prev = jnp.where(row == 0, carry, pltpu.roll(x, 1, axis=0))
        return x + (prev - x) * mu_ref[...]

    r = shifted(r_ref, rp_ref, mur_ref)
    k = shifted(k_ref, kp_ref, muk_ref)
    v = shifted(v_ref, vp_ref, muv_ref)
    lo = shifted(lo_ref, lop_ref, mulo_ref)
    cw, ca, cg = lo[:, :LANES], lo[:, LANES:2 * LANES], lo[:, 2 * LANES:]

    lw = -RWKV_DECAY_SCALE * jax.nn.sigmoid(w0_ref[...] + _dot(jnp.tanh(cw), w2_ref[...]))
    a = jax.nn.sigmoid(a0_ref[...] + _dot(ca, a2_ref[...]))
    g = _dot(jax.nn.sigmoid(cg), g2_ref[...])

    eye = li == lj
    strict_blk = (li % c) > (lj % c)
    incl_blk = (li % c) >= (lj % c)

    kk = k * kk_ref[...]
    kk = kk * lax.rsqrt(jnp.maximum(_dot_split(kk * kk, head_ones, exact='b'), 1e-24))
    k2 = k * (1.0 + (a - 1.0) * ka_ref[...])
    av = -kk
    bv = kk * a
    bonus = _dot_split(r * k2 * rk_ref[...], head_ones, exact='b') * v

    ci = lax.broadcasted_iota(I32, (c, c), 0)
    cj = lax.broadcasted_iota(I32, (c, c), 1)
    ltri = (cj <= ci).astype(F32)
    lane = lax.broadcasted_iota(I32, (c, LANES), 1)
    head0 = lane < hd

    def stack(x):
        return jnp.concatenate([jnp.where(head0, x, 0.0), jnp.where(head0, 0.0, x)], axis=0)

    eye_f = eye.astype(F32)
    incl2 = jnp.concatenate([incl_blk, incl_blk], axis=1)
    chain = {"st": st_ref[...], "ys": [], "todo": []}

    def recur():
        if chain["todo"]:
            qm, add = chain["todo"].pop(0)
            both = _dot(qm, chain["st"]) + add
            chain["st"] = both[LANES:]
            chain["ys"].append(both[:c] + both[c:LANES])

    def factors(grp):
        rows = [slice(n * c, (n + 1) * c) for n in grp]
        ix = range(len(grp))
        cum = [_dot_split(ltri, lw[rows[n]], exact='a') for n in ix]
        recur()
        tot = [cum[n][c - 1:c, :] for n in ix]
        e_neg = [jnp.exp(-cum[n]) for n in ix]
        e_end = [jnp.exp(tot[n] - cum[n]) for n in ix]
        r_s = [stack(r[rows[n]] * jnp.exp(cum[n])) for n in ix]
        a_s = [stack(av[rows[n]] * jnp.exp(cum[n] - lw[rows[n]])) for n in ix]
        b_s = [stack(bv[rows[n]] * e_neg[n]) for n in ix]
        k_s = [stack(k2[rows[n]] * e_neg[n]) for n in ix]
        bh_s = [stack(bv[rows[n]] * e_end[n]) for n in ix]
        kh_s = [stack(k2[rows[n]] * e_end[n]) for n in ix]
        v_s = [stack(v[rows[n]]) for n in ix]
        recur()

        big = [_dot(jnp.concatenate([a_s[n], r_s[n]], axis=0),
                    jnp.concatenate([b_s[n], k_s[n]], axis=0), _NT) for n in ix]
        recur()
        nn = [jnp.where(strict_blk, big[n][:LANES, :LANES], 0.0) for n in ix]
        ak = [jnp.where(strict_blk, big[n][:LANES, LANES:], 0.0) for n in ix]
        rbk = [jnp.where(incl2, big[n][LANES:, :], 0.0) for n in ix]

        tinv = [eye_f + nn[n] for n in ix]
        npow = [_dot(nn[n], nn[n]) for n in ix]
        for _ in range(4):
            recur()
            both = [_dot(jnp.concatenate([npow[n], tinv[n]], axis=0), npow[n]) for n in ix]
            npow = [both[n][:LANES] for n in ix]
            tinv = [tinv[n] + both[n][LANES:] for n in ix]
        tinv = [tinv[n] + _dot(tinv[n], npow[n]) for n in ix]
        while chain["todo"]:
            recur()

        akv = [_dot(ak[n], v_s[n]) for n in ix]
        p12 = [_dot(tinv[n], jnp.concatenate([a_s[n], akv[n]], axis=1)) for n in ix]
        pv = [jnp.concatenate([p12[n][:, LANES:], v_s[n]], axis=0) for n in ix]
        qm_a = [_dot(jnp.concatenate([rbk[n][:, :LANES], bh_s[n].T], axis=0), p12[n][:, :LANES])
                for n in ix]
        qm_b = [_dot(jnp.concatenate([rbk[n], jnp.concatenate([bh_s[n], kh_s[n]], axis=0).T],
                                     axis=0), pv[n]) for n in ix]
        q1 = [r_s[n] + qm_a[n][:LANES] for n in ix]
        m1 = [jnp.where(eye, jnp.exp(tot[n]), 0.0) + qm_a[n][LANES:] for n in ix]
        return [(jnp.concatenate([q1[n], m1[n]], axis=0), qm_b[n]) for n in ix]

    gsz = RWKV_GROUP
    for g0 in range(0, len(nch), gsz):
        chain["todo"] = factors(list(nch)[g0:g0 + gsz])
    while chain["todo"]:
        recur()
    st_ref[...] = chain["st"]

    y = jnp.concatenate(chain["ys"], axis=0)
    gmu = _dot_split(y, head_ones, exact='b') * (1.0 / hd)
    yc = y - gmu
    var = _dot_split(yc * yc, head_ones, exact='b') * (1.0 / hd)
    yn = yc * lax.rsqrt(var + RWKV_GN_EPS) * gnw_ref[...] + gnb_ref[...]
    o_ref[...] = ((yn + bonus) * g).astype(o_ref.dtype)


def rwkv7_mixer(feat, lora, mu, mu_lora, w0, a0, k_k, k_a, r_k, gn_w, gn_b, w2p, a2p, g2):
    bsz, s, _ = feat.shape
    tb = min(RWKV_TB, s)
    assert s % tb == 0 and tb % CHUNK == 0
    w = RWKV_WIDTH
    npair = w // LANES
    nlo = RWKV_LORA_COLS
    pr = tb // 8

    def cur(width, col):
        return pl.BlockSpec((None, tb, width), lambda b, p, t: (b, t, col(p)))

    def prv(width, col):
        return pl.BlockSpec((None, 8, width),
                            lambda b, p, t: (b, jnp.maximum(t * pr - 1, 0), col(p)))

    def par(arr_cols, col):
        return pl.BlockSpec((1, arr_cols), lambda b, p, t: (0, col(p)))

    cr, ck, cv = (lambda p: p), (lambda p: npair + p), (lambda p: 2 * npair + p)
    clo = lambda p: 0
    vec = lambda x: x.reshape(1, -1)
    in_specs = [
        cur(LANES, cr), cur(LANES, ck), cur(LANES, cv), cur(nlo, clo),
        prv(LANES, cr), prv(LANES, ck), prv(LANES, cv), prv(nlo, clo),
        par(LANES, cr), par(LANES, ck), par(LANES, cv), par(nlo, clo),
    ] + [par(LANES, cr)] * 7 + [
        pl.BlockSpec((LANES, LANES), lambda b, p, t: (0, p)),
        pl.BlockSpec((LANES, LANES), lambda b, p, t: (0, p)),
        pl.BlockSpec((LORA_G, LANES), lambda b, p, t: (0, p)),
    ]
    return pl.pallas_call(
        functools.partial(_rwkv_kernel, tb=tb),
        out_shape=jax.ShapeDtypeStruct((bsz, s, w), BF16),
        grid=(bsz, npair, s // tb),
        in_specs=in_specs,
        out_specs=pl.BlockSpec((None, tb, LANES), lambda b, p, t: (b, t, p)),
        scratch_shapes=[pltpu.VMEM((LANES, LANES), F32)],
        compiler_params=_cparams(("parallel", "parallel", "arbitrary")),
        name="rwkv7",
    )(feat, feat, feat, lora, feat, feat, feat, lora,
      vec(mu), vec(mu), vec(mu), vec(mu_lora),
      vec(w0), vec(a0), vec(k_k), vec(k_a), vec(r_k), vec(gn_w), vec(gn_b),
      w2p, a2p, g2)


def _pad_cols(x, n):
    return jnp.pad(x, ((0, 0), (0, n - x.shape[1])))


def _pad_rows(x, n):
    return jnp.pad(x, ((0, n - x.shape[0]), (0, 0)))


def _rwkv_lora_cols(x):
    c0 = 3 * RWKV_WIDTH
    cw = x[..., c0:c0 + LORA_W]
    ca = x[..., c0 + LORA_W:c0 + LORA_W + LORA_A]
    cg = x[..., c0 + LORA_W + LORA_A:]
    pad = lambda y: jnp.pad(y, [(0, 0)] * (y.ndim - 1) + [(0, LANES - y.shape[-1])])
    return jnp.concatenate([pad(cw), pad(ca), cg], axis=-1)


ROUTER_COLS = LANES


def _router_params(rg, rgb, re, reb):
    d = rg.shape[0]
    w = jnp.zeros((d, ROUTER_COLS), F32).at[:, :N_GROUPS].set(rg)
    w = w.at[:, N_GROUPS:N_GROUPS + N_EXPERTS].set(re)
    b = jnp.zeros((ROUTER_COLS,), F32).at[:N_GROUPS].set(rgb)
    b = b.at[N_GROUPS:N_GROUPS + N_EXPERTS].set(reb)
    return w, b


def _tail(h_f32, mix_parts, w_out, ln1_g, ln1_b, rg, rgb, re, reb, wg, wu, wd, ln2_g, ln2_b):
    mixed = proj(mix_parts, w_out, out_dtype=F32)
    rw, rb = _router_params(rg, rgb, re, reb)
    h1_f32, _, logits = ln_router(h_f32, mixed, ln1_g, ln1_b, rw, rb)
    return hier_moe_ln(h1_f32, logits, wg, wu, wd, ln2_g, ln2_b)


def kernel(x, l0_w_in, l0_shift_mu, l0_w0, l0_w2, l0_a0, l0_a2, l0_g2, l0_k_k, l0_k_a, l0_r_k, l0_gn_w, l0_gn_b, l0_w_out, l0_ln1_g, l0_ln1_b, l0_router_g, l0_router_g_b, l0_router_e, l0_router_e_b, l0_w_gate, l0_w_up, l0_w_down, l0_ln2_g, l0_ln2_b, l1_w_in, l1_lq1, l1_lk1, l1_lq2, l1_lk2, l1_subln_w, l1_gate_w2, l1_gate_b, l1_gla_norm_w, l1_w_out, l1_ln1_g, l1_ln1_b, l1_router_g, l1_router_g_b, l1_router_e, l1_router_e_b, l1_w_gate, l1_w_up, l1_w_down, l1_ln2_g, l1_ln2_b):
    bsz, s, d = x.shape
    t = bsz * s
    h0 = x.reshape(t, d)

    sbw = 3 * SB_WIDTH
    rww = 3 * RWKV_WIDTH
    x_bf16 = h0.astype(BF16)
    qkv0 = proj([x_bf16], l0_w_in, col0=0, ncols=sbw, out_dtype=BF16)
    feat = proj([x_bf16], l0_w_in, col0=sbw, ncols=rww, out_dtype=F32)
    lora = proj([x_bf16], _rwkv_lora_cols(l0_w_in[:, sbw:]), out_dtype=F32)
    o_sb = sb_attention(qkv0.reshape(bsz, s, sbw))
    o_rw = rwkv7_mixer(feat.reshape(bsz, s, rww), lora.reshape(bsz, s, RWKV_LORA_COLS),
                       l0_shift_mu[:rww], _rwkv_lora_cols(l0_shift_mu[None])[0],
                       l0_w0, l0_a0, l0_k_k, l0_k_a, l0_r_k.reshape(-1), l0_gn_w, l0_gn_b,
                       _pad_rows(l0_w2, LANES), _pad_rows(l0_a2, LANES), l0_g2)
    h2_f32, h2_bf16 = _tail(h0, [o_sb.reshape(t, SB_WIDTH), o_rw.reshape(t, RWKV_WIDTH)],
                            l0_w_out, l0_ln1_g, l0_ln1_b, l0_router_g, l0_router_g_b,
                            l0_router_e, l0_router_e_b, l0_w_gate, l0_w_up, l0_w_down,
                            l0_ln2_g, l0_ln2_b)

    dfw = 3 * DIFF_WIDTH
    glw = 2 * GLA_QK + GLA_WIDTH
    g0 = dfw + glw
    g1 = g0 + GLA_GATE_RANK
    qkv1 = proj([h2_bf16], l1_w_in, col0=0, ncols=dfw, out_dtype=BF16)
    gproj = proj([h2_bf16], l1_w_in, col0=dfw, ncols=glw, out_dtype=F32)
    og = proj([h2_bf16], l1_w_in[:, g1:], out_dtype=F32)
    cgp = proj_exact(h2_f32, _pad_cols(l1_w_in[:, g0:g1], LANES))
    lam_init = 0.8 - 0.6 * math.exp(-0.3 * 1)
    lam = (jnp.exp(jnp.sum(l1_lq1 * l1_lk1)) - jnp.exp(jnp.sum(l1_lq2 * l1_lk2)) + lam_init)
    o_diff = diff_attention(qkv1.reshape(bsz, s, dfw), lam, l1_subln_w, lam_init)
    o_gla = gla_mixer(gproj.reshape(bsz, s, glw), og.reshape(bsz, s, GLA_WIDTH),
                      cgp.reshape(bsz, s, LANES), _pad_rows(l1_gate_w2, LANES), l1_gate_b,
                      l1_gla_norm_w)
    out_f32, _ = _tail(h2_f32, [o_diff.reshape(t, DIFF_WIDTH), o_gla.reshape(t, GLA_WIDTH)],
                       l1_w_out, l1_ln1_g, l1_ln1_b, l1_router_g, l1_router_g_b,
                       l1_router_e, l1_router_e_b, l1_w_gate, l1_w_up, l1_w_down,
                       l1_ln2_g, l1_ln2_b)
    return out_f32.reshape(bsz, s, d)
```

```python
import functools
import math

import jax
import jax.numpy as jnp
from jax import lax
from jax.experimental import pallas as pl
from jax.experimental.pallas import tpu as pltpu

F32 = jnp.float32
BF16 = jnp.bfloat16
I32 = jnp.int32

D_MODEL = 4096
DEPTH = 2
CHUNK = 64
SB_WIDTH = 2048
SB_HEAD_DIM = 128
SB_HEADS = SB_WIDTH // SB_HEAD_DIM
RWKV_WIDTH = 2048
RWKV_HEAD_DIM = 64
LORA_W = 96
LORA_A = 96
LORA_G = 256
RWKV_DECAY_SCALE = math.exp(-0.5)
RWKV_GN_EPS = 64e-5
DIFF_WIDTH = 2048
DIFF_HEAD_DIM = 128
DIFF_HEADS = DIFF_WIDTH // (2 * DIFF_HEAD_DIM)
GLA_WIDTH = 2048
GLA_HEADS = 4
GLA_QK = GLA_WIDTH // 2
GLA_DK = GLA_QK // GLA_HEADS
GLA_DV = GLA_WIDTH // GLA_HEADS
GLA_GATE_RANK = 16
GLA_GATE_NORM = 16.0
N_GROUPS = 4
EXPERTS_PER_GROUP = 8
N_EXPERTS = N_GROUPS * EXPERTS_PER_GROUP
TOP_K = 2
D_EXPERT = 768
DEEPNORM_ALPHA = (2 * DEPTH) ** 0.25
LN_EPS = 1e-5

LANES = 128
V7X_VMEM_LIMIT = 56 * 1024 * 1024

def _cparams(semantics, vmem=V7X_VMEM_LIMIT):
    return pltpu.CompilerParams(dimension_semantics=semantics, vmem_limit_bytes=vmem)


def _dot(a, b, dims=(((1,), (0,)), ((), ()))):
    return lax.dot_general(a.astype(BF16), b.astype(BF16), dims, preferred_element_type=F32)


def _split2(x):
    hi = x.astype(BF16)
    return hi, (x - hi.astype(F32)).astype(BF16)


def _dot_split(a, b, *, exact, dims=(((1,), (0,)), ((), ()))):
    dn = lambda x, y: lax.dot_general(x, y, dims, preferred_element_type=F32)
    if exact == 'a':
        hi, lo = _split2(b)
        a16 = a.astype(BF16)
        return dn(a16, hi) + dn(a16, lo)
    hi, lo = _split2(a)
    b16 = b.astype(BF16)
    return dn(hi, b16) + dn(lo, b16)


def _dot3(a, b):
    ah, al = _split2(a)
    bh, bl = _split2(b)
    dn = lambda x, y: lax.dot_general(x, y, (((1,), (0,)), ((), ())), preferred_element_type=F32)
    return dn(ah, bh) + dn(ah, bl) + dn(al, bh)


_NT = (((1,), (1,)), ((), ()))
_TN = (((0,), (0,)), ((), ()))


PROJ_TM = 1024
PROJ_TN = 512


def _proj_kernel(*refs, nx):
    x_refs, w_ref, o_ref, wb_ref = refs[:nx], refs[nx], refs[nx + 1], refs[nx + 2]

    @pl.when(pl.program_id(1) == 0)
    def _():
        wb_ref[...] = w_ref[...].astype(BF16)

    acc = None
    k0 = 0
    for x_ref in x_refs:
        kx = x_ref.shape[1]
        part = lax.dot_general(x_ref[...], wb_ref[k0:k0 + kx, :], (((1,), (0,)), ((), ())),
                               preferred_element_type=F32)
        acc = part if acc is None else acc + part
        k0 += kx
    o_ref[...] = acc.astype(o_ref.dtype)


def proj(xs, w, *, col0=0, ncols=None, out_dtype):
    m = xs[0].shape[0]
    k = w.shape[0]
    ncols = w.shape[1] - col0 if ncols is None else ncols
    tm, tn = PROJ_TM, PROJ_TN
    assert sum(x.shape[1] for x in xs) == k and m % tm == 0
    assert ncols % tn == 0 and col0 % tn == 0, (col0, ncols)
    cb = col0 // tn
    in_specs = [pl.BlockSpec((tm, x.shape[1]), lambda j, i: (i, 0)) for x in xs]
    in_specs.append(pl.BlockSpec((k, tn), lambda j, i: (0, cb + j)))
    return pl.pallas_call(
        functools.partial(_proj_kernel, nx=len(xs)),
        out_shape=jax.ShapeDtypeStruct((m, ncols), out_dtype),
        grid=(ncols // tn, m // tm),
        in_specs=in_specs,
        out_specs=pl.BlockSpec((tm, tn), lambda j, i: (i, j)),
        scratch_shapes=[pltpu.VMEM((k, tn), BF16)],
        compiler_params=_cparams(("arbitrary", "arbitrary")),
        name="proj",
    )(*xs, w)


def _proj_exact_kernel(x_ref, w_ref, o_ref):
    o_ref[...] = _dot3(x_ref[...], w_ref[...])


def proj_exact(x, w, *, tm=512):
    m, k = x.shape
    n = w.shape[1]
    return pl.pallas_call(
        _proj_exact_kernel,
        out_shape=jax.ShapeDtypeStruct((m, n), F32),
        grid=(m // tm,),
        in_specs=[pl.BlockSpec((tm, k), lambda i: (i, 0)), pl.BlockSpec((k, n), lambda i: (0, 0))],
        out_specs=pl.BlockSpec((tm, n), lambda i: (i, 0)),
        compiler_params=_cparams(("parallel",)),
        name="proj_exact",
    )(x, w)


def _layer_norm_rows(x, g, b):
    mu = jnp.mean(x, axis=-1, keepdims=True)
    xc = x - mu
    var = jnp.mean(xc * xc, axis=-1, keepdims=True)
    return xc * lax.rsqrt(var + LN_EPS) * g + b


def _ln_router_kernel(h_ref, m_ref, g_ref, b_ref, r_ref, rb_ref, of_ref, lg_ref):
    y = _layer_norm_rows(DEEPNORM_ALPHA * h_ref[...] + m_ref[...].astype(F32),
                         g_ref[...], b_ref[...])
    of_ref[...] = y
    lg_ref[...] = _dot3(y, r_ref[...]) + rb_ref[...]


def ln_router(h, mixed, g, b, router_w, router_b, *, tm=256):
    t, d = h.shape
    nr = router_w.shape[1]
    row = lambda i: (i, 0)
    fixed = lambda i: (0, 0)
    return pl.pallas_call(
        _ln_router_kernel,
        out_shape=(jax.ShapeDtypeStruct((t, d), F32), jax.ShapeDtypeStruct((t, nr), F32)),
        grid=(t // tm,),
        in_specs=[pl.BlockSpec((tm, d), row), pl.BlockSpec((tm, d), row),
                  pl.BlockSpec((1, d), fixed), pl.BlockSpec((1, d), fixed),
                  pl.BlockSpec((d, nr), fixed), pl.BlockSpec((1, nr), fixed)],
        out_specs=(pl.BlockSpec((tm, d), row), pl.BlockSpec((tm, nr), row)),
        compiler_params=_cparams(("parallel",)),
        name="ln_router",
    )(h, mixed, g.reshape(1, d), b.reshape(1, d), router_w, router_b.reshape(1, nr))


MOE_ROWS = 256
MOE_UP_SPLIT = 2
MOE_TM = 256
U32 = jnp.uint32


def _pack_halves(y):
    n = y.shape[1] // 2
    bits = pltpu.bitcast(y, U32)
    rounded = bits + U32(0x7FFF) + ((bits >> 16) & U32(1))
    return (rounded[:, :n] & U32(0xFFFF0000)) | (rounded[:, n:] >> 16)


def _unpack_halves_f32(u):
    return pltpu.bitcast(u & U32(0xFFFF0000), F32), pltpu.bitcast(u << 16, F32)


def _unpack_halves(u):
    hi, lo = _unpack_halves_f32(u)
    return hi.astype(BF16), lo.astype(BF16)


DMA_UNROLL = 8

ET_NEXT, ET_SLOT, ET_LAST, ET_HAS, ET_NB = 0, 1, 2, 3, 4


def _dispatch_kernel(pos_ref, et_ref, h_ref, xs_out, buf, zbuf, sem, zsem, *, tm, nsteps):
    i = pl.program_id(0)
    slot = i % 2

    @pl.when(i == 0)
    def _():
        zbuf[...] = jnp.zeros(zbuf.shape, zbuf.dtype)

        def zero_copy(e):
            first = pl.multiple_of(et_ref[ET_LAST, e], MOE_ROWS)
            return pltpu.make_async_copy(zbuf, xs_out.at[pl.ds(first, MOE_ROWS), :], zsem)

        def unused_copy(blk):
            first = pl.multiple_of(blk * MOE_ROWS, MOE_ROWS)
            return pltpu.make_async_copy(zbuf, xs_out.at[pl.ds(first, MOE_ROWS), :], zsem)

        nblk = xs_out.shape[0] // MOE_ROWS
        nb = et_ref[ET_NB, 0]
        for e in range(N_EXPERTS):
            @pl.when(et_ref[ET_HAS, e] > 0)
            def _():
                zero_copy(e).start()
        lax.fori_loop(nb, nblk, lambda blk, c: (unused_copy(blk).start(), c)[1], 0)
        for e in range(N_EXPERTS):
            @pl.when(et_ref[ET_HAS, e] > 0)
            def _():
                zero_copy(e).wait()
        lax.fori_loop(nb, nblk, lambda blk, c: (unused_copy(blk).wait(), c)[1], 0)

    def row_copy(s, r, p):
        return pltpu.make_async_copy(buf.at[s, pl.ds(r, 1), :], xs_out.at[pl.ds(p, 1), :],
                                     sem.at[s])

    def drain(s):
        def body(r, c):
            for _ in range(TOP_K):
                row_copy(s, r, 0).wait()
            return c
        lax.fori_loop(0, tm, body, 0, unroll=DMA_UNROLL)

    @pl.when(i >= 2)
    def _():
        drain(slot)

    buf[slot] = _pack_halves(h_ref[...])

    def issue(r, c):
        a = (i * tm + r) * TOP_K
        for k in range(TOP_K):
            row_copy(slot, r, pos_ref[a + k]).start()
        return c
    lax.fori_loop(0, tm, issue, 0, unroll=DMA_UNROLL)

    @pl.when(i == nsteps - 1)
    def _():
        drain(slot)
        if nsteps > 1:
            drain(1 - slot)


def moe_dispatch(h, pos, etab, nslots):
    t, d = h.shape
    tm = MOE_TM
    nsteps = t // tm
    grid_spec = pltpu.PrefetchScalarGridSpec(
        num_scalar_prefetch=2,
        grid=(nsteps,),
        in_specs=[pl.BlockSpec((tm, d), lambda i, pos, et: (i, 0))],
        out_specs=pl.BlockSpec(memory_space=pl.ANY),
        scratch_shapes=[pltpu.VMEM((2, tm, d // 2), U32), pltpu.VMEM((MOE_ROWS, d // 2), U32),
                        pltpu.SemaphoreType.DMA((2,)), pltpu.SemaphoreType.DMA(())],
    )
    return pl.pallas_call(
        functools.partial(_dispatch_kernel, tm=tm, nsteps=nsteps),
        out_shape=jax.ShapeDtypeStruct((nslots, d // 2), U32),
        grid_spec=grid_spec,
        compiler_params=_cparams(("arbitrary",)),
        name="moe_dispatch",
    )(pos.reshape(-1), etab, h)


def _expert_changed(be_ref, i):
    return (i == 0) | (be_ref[i] != be_ref[jnp.maximum(i - 1, 0)])


def _expert_weights(be_ref, et_ref, i, copies, convert):
    e = be_ref[i]
    slot = et_ref[ET_SLOT, e]

    @pl.when(_expert_changed(be_ref, i))
    def _():
        @pl.when(i == 0)
        def _():
            for c in copies(e, slot):
                c.start()
        for c in copies(e, slot):
            c.wait()
        nxt = et_ref[ET_NEXT, e]

        @pl.when(nxt >= 0)
        def _():
            for c in copies(nxt, 1 - slot):
                c.start()
        convert(slot)


def _moe_up_kernel(be_ref, nb_ref, et_ref, xs_ref, wg_hbm, wu_hbm, a_ref, wst, wc_ref, sem):
    j = pl.program_id(0)
    i = pl.program_id(1)
    half = xs_ref.shape[1]
    ce = a_ref.shape[1]

    def copies(e, slot):
        cols = pl.ds(pl.multiple_of(j * ce, LANES), ce)
        return [pltpu.make_async_copy(wg_hbm.at[e, :, cols], wst.at[slot, 0], sem.at[slot]),
                pltpu.make_async_copy(wu_hbm.at[e, :, cols], wst.at[slot, 1], sem.at[slot])]

    def convert(slot):
        for hh in range(2):
            rows = slice(hh * half, (hh + 1) * half)
            wc_ref[hh] = jnp.concatenate([wst[slot, 0, rows, :], wst[slot, 1, rows, :]],
                                         axis=1).astype(BF16)

    @pl.when(i < nb_ref[0])
    def _():
        _expert_weights(be_ref, et_ref, i, copies, convert)
        hi, lo = _unpack_halves(xs_ref[...])
        hgu = _dot(hi, wc_ref[0]) + _dot(lo, wc_ref[1])
        hg, hu = hgu[:, :ce], hgu[:, ce:]
        a_ref[...] = (hg * jax.nn.sigmoid(hg) * hu).astype(a_ref.dtype)

    @pl.when(i >= nb_ref[0])
    def _():
        a_ref[...] = jnp.zeros_like(a_ref)


def moe_up(xs, block_e, nb_used, etab, wg, wu):
    nslots, half = xs.shape
    _, d, de = wg.shape
    rows = MOE_ROWS
    nblk = nslots // rows
    ce = de // MOE_UP_SPLIT
    last = lambda i, nb: jnp.minimum(i, nb[0] - 1)
    grid_spec = pltpu.PrefetchScalarGridSpec(
        num_scalar_prefetch=3,
        grid=(MOE_UP_SPLIT, nblk),
        in_specs=[
            pl.BlockSpec((rows, half), lambda j, i, be, nb, et: (last(i, nb), 0)),
            pl.BlockSpec(memory_space=pl.ANY),
            pl.BlockSpec(memory_space=pl.ANY),
        ],
        out_specs=pl.BlockSpec((rows, ce), lambda j, i, be, nb, et: (i, j)),
        scratch_shapes=[pltpu.VMEM((2, 2, d, ce), F32), pltpu.VMEM((2, half, 2 * ce), BF16),
                        pltpu.SemaphoreType.DMA((2,))],
    )
    return pl.pallas_call(
        _moe_up_kernel,
        out_shape=jax.ShapeDtypeStruct((nslots, de), BF16),
        grid_spec=grid_spec,
        compiler_params=_cparams(("arbitrary", "arbitrary")),
        name="moe_up",
    )(block_e, nb_used, etab, xs, wg, wu)


def _moe_down_kernel(be_ref, nb_ref, et_ref, a_ref, wd_hbm, o_ref, wst, wc_ref, sem):
    i = pl.program_id(0)

    def copies(e, slot):
        return [pltpu.make_async_copy(wd_hbm.at[e], wst.at[slot], sem.at[slot])]

    def convert(slot):
        wc_ref[...] = wst[slot].astype(BF16)

    @pl.when(i < nb_ref[0])
    def _():
        _expert_weights(be_ref, et_ref, i, copies, convert)
        o_ref[...] = _pack_halves(_dot(a_ref[...], wc_ref[...]))

    @pl.when(i >= nb_ref[0])
    def _():
        o_ref[...] = jnp.zeros_like(o_ref)


def moe_down(act, block_e, nb_used, etab, wd):
    nslots, de = act.shape
    d = wd.shape[2]
    rows = MOE_ROWS
    nblk = nslots // rows
    last = lambda i, nb: jnp.minimum(i, nb[0] - 1)
    grid_spec = pltpu.PrefetchScalarGridSpec(
        num_scalar_prefetch=3,
        grid=(nblk,),
        in_specs=[
            pl.BlockSpec((rows, de), lambda i, be, nb, et: (last(i, nb), 0)),
            pl.BlockSpec(memory_space=pl.ANY),
        ],
        out_specs=pl.BlockSpec((rows, d // 2), lambda i, be, nb, et: (i, 0)),
        scratch_shapes=[pltpu.VMEM((2, de, d), F32), pltpu.VMEM((de, d), BF16),
                        pltpu.SemaphoreType.DMA((2,))],
    )
    return pl.pallas_call(
        _moe_down_kernel,
        out_shape=jax.ShapeDtypeStruct((nslots, d // 2), U32),
        grid_spec=grid_spec,
        compiler_params=_cparams(("arbitrary",)),
        name="moe_down",
    )(block_e, nb_used, etab, act, wd)


def _combine_ln_kernel(pos_ref, h_ref, gate_ref, yb_hbm, g_ref, b_ref, of_ref, ob_ref,
                       ybuf, sem, *, tm, nsteps):
    i = pl.program_id(0)
    slot = i % 2

    def row_copy(s, k, r, p):
        return pltpu.make_async_copy(yb_hbm.at[pl.ds(p, 1), :], ybuf.at[s, k, pl.ds(r, 1), :],
                                     sem.at[s])

    def issue(step, s):
        def body(r, c):
            a = (step * tm + r) * TOP_K
            for k in range(TOP_K):
                row_copy(s, k, r, pos_ref[a + k]).start()
            return c
        lax.fori_loop(0, tm, body, 0, unroll=DMA_UNROLL)

    @pl.when(i == 0)
    def _():
        issue(0, 0)

    @pl.when(i + 1 < nsteps)
    def _():
        issue(i + 1, 1 - slot)

    def drain(r, c):
        for k in range(TOP_K):
            row_copy(slot, k, r, 0).wait()
        return c
    lax.fori_loop(0, tm, drain, 0, unroll=DMA_UNROLL)

    gate = gate_ref[...]
    y_hi = y_lo = None
    for k in range(TOP_K):
        hi, lo = _unpack_halves_f32(ybuf[slot, k])
        y_hi = hi * gate[:, k:k + 1] if y_hi is None else y_hi + hi * gate[:, k:k + 1]
        y_lo = lo * gate[:, k:k + 1] if y_lo is None else y_lo + lo * gate[:, k:k + 1]
    y = jnp.concatenate([y_hi, y_lo], axis=1)
    out = _layer_norm_rows(DEEPNORM_ALPHA * h_ref[...] + y, g_ref[...], b_ref[...])
    of_ref[...] = out
    ob_ref[...] = out.astype(BF16)


def combine_ln(h, gates, pos, yb, g, b):
    t, d = h.shape
    tm = MOE_TM
    nsteps = t // tm
    grid_spec = pltpu.PrefetchScalarGridSpec(
        num_scalar_prefetch=1,
        grid=(nsteps,),
        in_specs=[
            pl.BlockSpec((tm, d), lambda i, pos: (i, 0)),
            pl.BlockSpec((tm, TOP_K), lambda i, pos: (i, 0)),
            pl.BlockSpec(memory_space=pl.ANY),
            pl.BlockSpec((1, d), lambda i, pos: (0, 0)),
            pl.BlockSpec((1, d), lambda i, pos: (0, 0)),
        ],
        out_specs=(pl.BlockSpec((tm, d), lambda i, pos: (i, 0)),
                   pl.BlockSpec((tm, d), lambda i, pos: (i, 0))),
        scratch_shapes=[pltpu.VMEM((2, TOP_K, tm, d // 2), U32), pltpu.SemaphoreType.DMA((2,))],
    )
    return pl.pallas_call(
        functools.partial(_combine_ln_kernel, tm=tm, nsteps=nsteps),
        out_shape=(jax.ShapeDtypeStruct((t, d), F32), jax.ShapeDtypeStruct((t, d), BF16)),
        grid_spec=grid_spec,
        compiler_params=_cparams(("arbitrary",)),
        name="combine_ln",
    )(pos.reshape(-1), h, gates, yb, g.reshape(1, d), b.reshape(1, d))


def _routing_tables(logits, t):
    g_logits = logits[:, :N_GROUPS]
    g_prob = jax.nn.softmax(g_logits, axis=-1)
    g_sel = jnp.argmax(g_logits, axis=-1)
    g_w = jnp.take_along_axis(g_prob, g_sel[:, None], axis=-1)[:, 0]
    e_logits = logits[:, N_GROUPS:N_GROUPS + N_EXPERTS].reshape(t, N_GROUPS, EXPERTS_PER_GROUP)
    e_in_group = jnp.take_along_axis(e_logits, g_sel[:, None, None], axis=1)[:, 0]
    top_vals, top_idx = lax.top_k(e_in_group, TOP_K)
    gates = jax.nn.softmax(top_vals, axis=-1) * g_w[:, None]
    expert_id = (g_sel[:, None] * EXPERTS_PER_GROUP + top_idx).astype(I32)

    a = t * TOP_K
    rows = MOE_ROWS
    flat_e = expert_id.reshape(a)
    onehot = (flat_e[:, None] == jnp.arange(N_EXPERTS, dtype=I32)[None, :]).astype(I32)
    seen = jnp.cumsum(onehot, axis=0)
    counts = seen[-1]
    padded = (counts + rows - 1) // rows * rows
    pad_end = jnp.cumsum(padded)
    pad_start = pad_end - padded
    pos = jnp.sum(onehot * (seen - 1 + pad_start[None, :]), axis=1).astype(I32)
    nblk = a // rows + N_EXPERTS
    block_start = jnp.arange(nblk, dtype=I32) * rows
    block_e = jnp.minimum(jnp.sum((pad_end[None, :] <= block_start[:, None]).astype(I32), axis=1),
                          N_EXPERTS - 1).astype(I32)
    nb_used = (pad_end[-1] // rows).astype(I32).reshape(1)
    eidx = jnp.arange(N_EXPERTS, dtype=I32)
    has = counts > 0
    from_here = lax.cummin(jnp.where(has, eidx, N_EXPERTS), reverse=True)
    nxt = jnp.concatenate([from_here[1:], jnp.full((1,), N_EXPERTS, I32)])
    nxt = jnp.where(nxt >= N_EXPERTS, -1, nxt)
    run_slot = (jnp.cumsum(has.astype(I32)) - has.astype(I32)) % 2
    last_first = jnp.where(has, pad_end - rows, 0)
    etab = jnp.stack([nxt, run_slot, last_first, has.astype(I32),
                      jnp.broadcast_to(nb_used, (N_EXPERTS,))]).astype(I32)
    return gates.astype(F32), pos.reshape(t, TOP_K), block_e, nb_used, etab, nblk * rows


def hier_moe_ln(h_f32, logits, wg, wu, wd, ln_g, ln_b):
    t = h_f32.shape[0]
    gates, pos, block_e, nb_used, etab, nslots = _routing_tables(logits, t)
    xs = moe_dispatch(h_f32, pos, etab, nslots)
    act = moe_up(xs, block_e, nb_used, etab, wg, wu)
    yb = moe_down(act, block_e, nb_used, etab, wd)
    return combine_ln(h_f32, gates, pos, yb, ln_g, ln_b)


DIFF_T = 512
NEG_BIG = -1e30


def _lane_tile(x, n):
    return x if n == 1 else jnp.concatenate([x] * n, axis=1)


DIFF_HEADS_PER_STEP = 2


def _diff_attn_kernel(lam_ref, q_ref, k_ref, v_ref, sw_ref, o_ref,
                      m_ref, l_ref, acc_ref, *, t, nhs, scale, out_scale):
    i = pl.program_id(2)
    d = DIFF_HEAD_DIM
    chains = range(2 * nhs)
    qk_cols = [slice(c * d, (c + 1) * d) for c in chains]
    v_cols = [slice(c // 2 * 2 * d, (c // 2 + 1) * 2 * d) for c in chains]
    qs = [(q_ref[:, qk_cols[c]].astype(F32) * scale).astype(BF16) for c in chains]
    m_ref[...] = jnp.full(m_ref.shape, NEG_BIG, F32)
    l_ref[...] = jnp.zeros(l_ref.shape, F32)
    acc_ref[...] = jnp.zeros(acc_ref.shape, F32)
    krep = t // LANES
    vrep = 2 * d // LANES

    def block(j, diagonal):
        start = pl.multiple_of(j * t, t)
        s = [lax.dot_general(qs[c], k_ref[pl.ds(start, t), qk_cols[c]], _NT,
                             preferred_element_type=F32) for c in chains]
        if diagonal:
            row_chunk = lax.broadcasted_iota(I32, (t, t), 0) // CHUNK
            col_chunk = lax.broadcasted_iota(I32, (t, t), 1) // CHUNK
            s = [jnp.where(col_chunk <= row_chunk, s[c], NEG_BIG) for c in chains]
        m_prev = [m_ref[c] for c in chains]
        m_new = [jnp.maximum(m_prev[c], jnp.max(s[c], axis=1, keepdims=True)) for c in chains]
        p = [jnp.exp(s[c] - _lane_tile(m_new[c], krep)) for c in chains]
        alpha = [jnp.exp(m_prev[c] - m_new[c]) for c in chains]
        l_new = [alpha[c] * l_ref[c] + jnp.sum(p[c], axis=1, keepdims=True) for c in chains]
        pv = [_dot(p[c], v_ref[pl.ds(start, t), v_cols[c]]) for c in chains]
        for c in chains:
            acc_ref[c] = _lane_tile(alpha[c], vrep) * acc_ref[c] + pv[c]
            m_ref[c] = m_new[c]
            l_ref[c] = l_new[c]

    def full_block(j, carry):
        block(j, False)
        return carry

    lax.fori_loop(0, i, full_block, 0)
    block(i, True)
    lam = lam_ref[0]
    for h in range(nhs):
        inv = [_lane_tile(1.0 / l_ref[2 * h + b], vrep) for b in range(2)]
        o = acc_ref[2 * h] * inv[0] - lam * (acc_ref[2 * h + 1] * inv[1])
        ms = jnp.mean(o * o, axis=-1, keepdims=True)
        o_ref[:, v_cols[2 * h]] = (o * lax.rsqrt(ms + 1e-5) * sw_ref[...]
                                   * out_scale).astype(o_ref.dtype)


def diff_attention(qkv, lam, subln_w, lam_init):
    bsz, s, _ = qkv.shape
    tq = min(DIFF_T, s)
    assert s % tq == 0 and tq % CHUNK == 0
    d = DIFF_HEAD_DIM
    nhs = DIFF_HEADS_PER_STEP
    w = nhs * 2 * d
    ng = DIFF_WIDTH // w
    grid_spec = pltpu.PrefetchScalarGridSpec(
        num_scalar_prefetch=0,
        grid=(bsz, ng, s // tq),
        in_specs=[
            pl.BlockSpec(memory_space=pltpu.SMEM),
            pl.BlockSpec((None, tq, w), lambda b, g, i: (b, i, g)),
            pl.BlockSpec((None, s, w), lambda b, g, i: (b, 0, ng + g)),
            pl.BlockSpec((None, s, w), lambda b, g, i: (b, 0, 2 * ng + g)),
            pl.BlockSpec((1, 2 * d), lambda b, g, i: (0, 0)),
        ],
        out_specs=pl.BlockSpec((None, tq, w), lambda b, g, i: (b, i, g)),
        scratch_shapes=[pltpu.VMEM((2 * nhs, tq, LANES), F32),
                        pltpu.VMEM((2 * nhs, tq, LANES), F32),
                        pltpu.VMEM((2 * nhs, tq, 2 * d), F32)],
    )
    return pl.pallas_call(
        functools.partial(_diff_attn_kernel, t=tq, nhs=nhs, scale=d ** -0.5,
                          out_scale=1.0 - lam_init),
        out_shape=jax.ShapeDtypeStruct((bsz, s, DIFF_WIDTH), BF16),
        grid_spec=grid_spec,
        compiler_params=_cparams(("parallel", "parallel", "arbitrary")),
        name="diff_attn",
    )(lam.reshape(1).astype(F32), qkv, qkv, qkv, subln_w.reshape(1, 2 * d))


GLA_TB = 512


def _log_sigmoid(x):
    return jnp.minimum(x, 0.0) - jnp.log(1.0 + jnp.exp(-jnp.abs(x)))


def _gla_kernel(q_ref, k_ref, v_ref, og_ref, cg_ref, gw_ref, gb_ref, nw_ref, o_ref,
                state_ref, *, tb, scale):
    t = pl.program_id(2)

    @pl.when(t == 0)
    def _():
        state_ref[...] = jnp.zeros(state_ref.shape, F32)

    c = CHUNK
    ri = lax.broadcasted_iota(I32, (c, c), 0)
    ci = lax.broadcasted_iota(I32, (c, c), 1)
    causal = ci <= ri
    ltri = causal.astype(F32)
    ones_cols = jnp.ones((c, LANES), F32)
    dv = v_ref.shape[-1]
    nch = range(tb // c)
    rows = [slice(n * c, (n + 1) * c) for n in nch]
    gl = [_dot3(cg_ref[rows[n], :], gw_ref[...]) + gb_ref[...] for n in nch]
    log_a = [_log_sigmoid(gl[n]) * (1.0 / GLA_GATE_NORM) for n in nch]
    bcum = [_dot_split(ltri, log_a[n], exact='a') for n in nch]
    v = [v_ref[rows[n], :] for n in nch]
    q_dec = [q_ref[rows[n], :] * scale * jnp.exp(bcum[n]) for n in nch]
    k_inv = [k_ref[rows[n], :] * jnp.exp(-bcum[n]) for n in nch]
    k_end = [k_ref[rows[n], :] * jnp.exp(bcum[n][c - 1:c, :] - bcum[n]) for n in nch]
    scores = [jnp.where(causal, _dot(q_dec[n], k_inv[n], _NT), 0.0) for n in nch]
    o_intra = [_dot(scores[n], v[n]) for n in nch]
    dec = [_lane_tile(jnp.exp(_dot_split(log_a[n], ones_cols, exact='b', dims=_TN)),
                      dv // LANES) for n in nch]
    kv = [_dot(k_end[n].T, v[n]) for n in nch]

    state = state_ref[...]
    o = []
    for n in nch:
        o.append(o_intra[n] + _dot(q_dec[n], state))
        state = state * dec[n] + kv[n]
    state_ref[...] = state

    for n in nch:
        ms = jnp.mean(o[n] * o[n], axis=-1, keepdims=True)
        og = og_ref[rows[n], :]
        gate = og * jax.nn.sigmoid(og)
        o_ref[rows[n], :] = (o[n] * lax.rsqrt(ms + 1e-5) * nw_ref[...] * gate).astype(o_ref.dtype)


def gla_mixer(proj, og, cg, gate_w2p, gate_b, norm_w):
    bsz, s, _ = proj.shape
    tb = min(GLA_TB, s)
    assert s % tb == 0 and tb % CHUNK == 0
    dk, dv, nh = GLA_DK, GLA_DV, GLA_HEADS
    in_specs = [
        pl.BlockSpec((None, tb, dk), lambda b, h, t: (b, t, h)),
        pl.BlockSpec((None, tb, dk), lambda b, h, t: (b, t, nh + h)),
        pl.BlockSpec((None, tb, dv), lambda b, h, t: (b, t, nh + h)),
        pl.BlockSpec((None, tb, dv), lambda b, h, t: (b, t, h)),
        pl.BlockSpec((None, tb, LANES), lambda b, h, t: (b, t, 0)),
        pl.BlockSpec((LANES, dk), lambda b, h, t: (0, h)),
        pl.BlockSpec((1, dk), lambda b, h, t: (0, h)),
        pl.BlockSpec((1, dv), lambda b, h, t: (0, 0)),
    ]
    return pl.pallas_call(
        functools.partial(_gla_kernel, tb=tb, scale=dk ** -0.5),
        out_shape=jax.ShapeDtypeStruct((bsz, s, GLA_WIDTH), BF16),
        grid=(bsz, nh, s // tb),
        in_specs=in_specs,
        out_specs=pl.BlockSpec((None, tb, dv), lambda b, h, t: (b, t, h)),
        scratch_shapes=[pltpu.VMEM((dk, dv), F32)],
        compiler_params=_cparams(("parallel", "parallel", "arbitrary")),
        name="gla",
    )(proj, proj, proj, og, cg, gate_w2p, gate_b.reshape(1, GLA_QK), norm_w.reshape(1, dv))


SB_T = 256
SB_HEADS_PER_STEP = 4
SB_UNDERFLOW = 110.0


def _sb_attn_kernel(q_ref, k_ref, v_ref, u_ref, o_ref, acc_ref, run_ref, *, t, nhs, scale):
    i = pl.program_id(2)
    d = SB_HEAD_DIM
    heads = range(nhs)
    cols = [slice(h * d, (h + 1) * d) for h in heads]
    q = [(q_ref[:, cols[h]].astype(F32) * scale).astype(BF16) for h in heads]
    u = u_ref[...]
    rep = t // LANES

    def block(j, diagonal, first):
        start = pl.multiple_of(j * t, t)
        kb = [k_ref[pl.ds(start, t), cols[h]] for h in heads]
        vb = [v_ref[pl.ds(start, t), cols[h]] for h in heads]
        z = [lax.dot_general(q[h], kb[h], _NT, preferred_element_type=F32) for h in heads]
        sp = [jnp.maximum(z[h], 0.0) + jnp.log(1.0 + jnp.exp(-jnp.abs(z[h]))) for h in heads]
        if diagonal:
            strict = (lax.broadcasted_iota(I32, (t, t), 1) < lax.broadcasted_iota(I32, (t, t), 0))
            sp = [jnp.where(strict, sp[h], 0.0) for h in heads]
        hi = [sp[h].astype(BF16) for h in heads]
        lo = [(sp[h] - hi[h].astype(F32)).astype(BF16) for h in heads]
        sums = [lax.dot_general(hi[h], u, (((1,), (0,)), ((), ())), preferred_element_type=F32)
                + lax.dot_general(lo[h], u, (((1,), (0,)), ((), ())), preferred_element_type=F32)
                for h in heads]
        if first:
            x = [z[h] + sums[h] for h in heads]
        else:
            x = [z[h] + sums[h] + _lane_tile(run_ref[h], rep) for h in heads]
        w = [jnp.exp(x[h]) for h in heads]
        if diagonal:
            w = [jnp.where(strict, w[h], 0.0) for h in heads]
        pv = [_dot(w[h], vb[h]) for h in heads]
        total = [jnp.broadcast_to(sums[h][:, 0:1], (t, LANES)) for h in heads]
        run_max = None
        for h in heads:
            if first:
                acc_ref[:, cols[h]] = pv[h]
                run_new = total[h]
            else:
                acc_ref[:, cols[h]] += pv[h]
                run_new = run_ref[h] + total[h]
            run_ref[h] = run_new
            m = jnp.max(run_new)
            run_max = m if run_max is None else jnp.maximum(run_max, m)
        return (run_max > -SB_UNDERFLOW).astype(I32)

    alive0 = block(i, True, True)

    def cond(c):
        return (c[0] <= i) & (c[1] > 0)

    def body(c):
        return c[0] + 1, block(i - c[0], False, False)

    lax.while_loop(cond, body, (jnp.int32(1), alive0))
    o_ref[...] = acc_ref[...].astype(o_ref.dtype)


def sb_attention(qkv):
    bsz, s, _ = qkv.shape
    t = min(SB_T, s)
    assert s % t == 0
    nhs = SB_HEADS_PER_STEP
    w = nhs * SB_HEAD_DIM
    ng = SB_HEADS // nhs
    r = lax.broadcasted_iota(I32, (t, t), 0)
    c = lax.broadcasted_iota(I32, (t, t), 1)
    u = jnp.where(r >= c, -1.0, 0.0).astype(BF16)
    return pl.pallas_call(
        functools.partial(_sb_attn_kernel, t=t, nhs=nhs, scale=SB_HEAD_DIM ** -0.5),
        out_shape=jax.ShapeDtypeStruct((bsz, s, SB_WIDTH), BF16),
        grid=(bsz, ng, s // t),
        in_specs=[
            pl.BlockSpec((None, t, w), lambda b, g, i: (b, i, g)),
            pl.BlockSpec((None, s, w), lambda b, g, i: (b, 0, ng + g)),
            pl.BlockSpec((None, s, w), lambda b, g, i: (b, 0, 2 * ng + g)),
            pl.BlockSpec((t, t), lambda b, g, i: (0, 0)),
        ],
        out_specs=pl.BlockSpec((None, t, w), lambda b, g, i: (b, i, g)),
        scratch_shapes=[pltpu.VMEM((t, w), F32), pltpu.VMEM((nhs, t, LANES), F32)],
        compiler_params=_cparams(("parallel", "parallel", "arbitrary")),
        name="sb_attn",
    )(qkv, qkv, qkv, u)


RWKV_TB = 2048
RWKV_GROUP = 8
RWKV_PAIR = LANES // RWKV_HEAD_DIM
RWKV_LORA_COLS = 512


def _rwkv_kernel(r_ref, k_ref, v_ref, lo_ref, rp_ref, kp_ref, vp_ref, lop_ref,
                 mur_ref, muk_ref, muv_ref, mulo_ref,
                 w0_ref, a0_ref, kk_ref, ka_ref, rk_ref, gnw_ref, gnb_ref,
                 w2_ref, a2_ref, g2_ref, o_ref, st_ref, *, tb):
    t = pl.program_id(2)
    c = CHUNK
    hd = RWKV_HEAD_DIM
    nch = range(tb // c)

    @pl.when(t == 0)
    def _():
        st_ref[...] = jnp.zeros(st_ref.shape, F32)

    li = lax.broadcasted_iota(I32, (LANES, LANES), 0)
    lj = lax.broadcasted_iota(I32, (LANES, LANES), 1)
    head_ones = (li // hd == lj // hd).astype(F32)

    def shifted(x_ref, p_ref, mu_ref):
        x = x_ref[...]
        carry = jnp.where(t == 0, 0.0, p_ref[7:8, :])
        row = lax.broadcasted_iota(I32, x.shape, 0)
        prev = jnp.where(row == 0, carry, pltpu.roll(x, 1, axis=0))
        return x + (prev - x) * mu_ref[...]

    r = shifted(r_ref, rp_ref, mur_ref)
    k = shifted(k_ref, kp_ref, muk_ref)
    v = shifted(v_ref, vp_ref, muv_ref)
    lo = shifted(lo_ref, lop_ref, mulo_ref)
    cw, ca, cg = lo[:, :LANES], lo[:, LANES:2 * LANES], lo[:, 2 * LANES:]

    lw = -RWKV_DECAY_SCALE * jax.nn.sigmoid(w0_ref[...] + _dot(jnp.tanh(cw), w2_ref[...]))
    a = jax.nn.sigmoid(a0_ref[...] + _dot(ca, a2_ref[...]))
    g = _dot(jax.nn.sigmoid(cg), g2_ref[...])

    eye = li == lj
    strict_blk = (li % c) > (lj % c)
    incl_blk = (li % c) >= (lj % c)

    kk = k * kk_ref[...]
    kk = kk * lax.rsqrt(jnp.maximum(_dot_split(kk * kk, head_ones, exact='b'), 1e-24))
    k2 = k * (1.0 + (a - 1.0) * ka_ref[...])
    av = -kk
    bv = kk * a
    bonus = _dot_split(r * k2 * rk_ref[...], head_ones, exact='b') * v

    ci = lax.broadcasted_iota(I32, (c, c), 0)
    cj = lax.broadcasted_iota(I32, (c, c), 1)
    ltri = (cj <= ci).astype(F32)
    lane = lax.broadcasted_iota(I32, (c, LANES), 1)
    head0 = lane < hd

    def stack(x):
        return jnp.concatenate([jnp.where(head0, x, 0.0), jnp.where(head0, 0.0, x)], axis=0)

    eye_f = eye.astype(F32)
    incl2 = jnp.concatenate([incl_blk, incl_blk], axis=1)
    chain = {"st": st_ref[...], "ys": [], "todo": []}

    def recur():
        if chain["todo"]:
            qm, add = chain["todo"].pop(0)
            both = _dot(qm, chain["st"]) + add
            chain["st"] = both[LANES:]
            chain["ys"].append(both[:c] + both[c:LANES])

    def factors(grp):
        rows = [slice(n * c, (n + 1) * c) for n in grp]
        ix = range(len(grp))
        cum = [_dot_split(ltri, lw[rows[n]], exact='a') for n in ix]
        recur()
        tot = [cum[n][c - 1:c, :] for n in ix]
        e_neg = [jnp.exp(-cum[n]) for n in ix]
        e_end = [jnp.exp(tot[n] - cum[n]) for n in ix]
        r_s = [stack(r[rows[n]] * jnp.exp(cum[n])) for n in ix]
        a_s = [stack(av[rows[n]] * jnp.exp(cum[n] - lw[rows[n]])) for n in ix]
        b_s = [stack(bv[rows[n]] * e_neg[n]) for n in ix]
        k_s = [stack(k2[rows[n]] * e_neg[n]) for n in ix]
        bh_s = [stack(bv[rows[n]] * e_end[n]) for n in ix]
        kh_s = [stack(k2[rows[n]] * e_end[n]) for n in ix]
        v_s = [stack(v[rows[n]]) for n in ix]
        recur()

        big = [_dot(jnp.concatenate([a_s[n], r_s[n]], axis=0),
                    jnp.concatenate([b_s[n], k_s[n]], axis=0), _NT) for n in ix]
        recur()
        nn = [jnp.where(strict_blk, big[n][:LANES, :LANES], 0.0) for n in ix]
        ak = [jnp.where(strict_blk, big[n][:LANES, LANES:], 0.0) for n in ix]
        rbk = [jnp.where(incl2, big[n][LANES:, :], 0.0) for n in ix]

        tinv = [eye_f + nn[n] for n in ix]
        npow = [_dot(nn[n], nn[n]) for n in ix]
        for _ in range(4):
            recur()
            both = [_dot(jnp.concatenate([npow[n], tinv[n]], axis=0), npow[n]) for n in ix]
            npow = [both[n][:LANES] for n in ix]
            tinv = [tinv[n] + both[n][LANES:] for n in ix]
        tinv = [tinv[n] + _dot(tinv[n], npow[n]) for n in ix]
        while chain["todo"]:
            recur()

        akv = [_dot(ak[n], v_s[n]) for n in ix]
        p12 = [_dot(tinv[n], jnp.concatenate([a_s[n], akv[n]], axis=1)) for n in ix]
        pv = [jnp.concatenate([p12[n][:, LANES:], v_s[n]], axis=0) for n in ix]
        qm_a = [_dot(jnp.concatenate([rbk[n][:, :LANES], bh_s[n].T], axis=0), p12[n][:, :LANES])
                for n in ix]
        qm_b = [_dot(jnp.concatenate([rbk[n], jnp.concatenate([bh_s[n], kh_s[n]], axis=0).T],
                                     axis=0), pv[n]) for n in ix]
        q1 = [r_s[n] + qm_a[n][:LANES] for n in ix]
        m1 = [jnp.where(eye, jnp.exp(tot[n]), 0.0) + qm_a[n][LANES:] for n in ix]
        return [(jnp.concatenate([q1[n], m1[n]], axis=0), qm_b[n]) for n in ix]

    gsz = RWKV_GROUP
    for g0 in range(0, len(nch), gsz):
        chain["todo"] = factors(list(nch)[g0:g0 + gsz])
    while chain["todo"]:
        recur()
    st_ref[...] = chain["st"]

    y = jnp.concatenate(chain["ys"], axis=0)
    gmu = _dot_split(y, head_ones, exact='b') * (1.0 / hd)
    yc = y - gmu
    var = _dot_split(yc * yc, head_ones, exact='b') * (1.0 / hd)
    yn = yc * lax.rsqrt(var + RWKV_GN_EPS) * gnw_ref[...] + gnb_ref[...]
    o_ref[...] = ((yn + bonus) * g).astype(o_ref.dtype)


def rwkv7_mixer(feat, lora, mu, mu_lora, w0, a0, k_k, k_a, r_k, gn_w, gn_b, w2p, a2p, g2):
    bsz, s, _ = feat.shape
    tb = min(RWKV_TB, s)
    assert s % tb == 0 and tb % CHUNK == 0
    w = RWKV_WIDTH
    npair = w // LANES
    nlo = RWKV_LORA_COLS
    pr = tb // 8

    def cur(width, col):
        return pl.BlockSpec((None, tb, width), lambda b, p, t: (b, t, col(p)))

    def prv(width, col):
        return pl.BlockSpec((None, 8, width),
                            lambda b, p, t: (b, jnp.maximum(t * pr - 1, 0), col(p)))

    def par(arr_cols, col):
        return pl.BlockSpec((1, arr_cols), lambda b, p, t: (0, col(p)))

    cr, ck, cv = (lambda p: p), (lambda p: npair + p), (lambda p: 2 * npair + p)
    clo = lambda p: 0
    vec = lambda x: x.reshape(1, -1)
    in_specs = [
        cur(LANES, cr), cur(LANES, ck), cur(LANES, cv), cur(nlo, clo),
        prv(LANES, cr), prv(LANES, ck), prv(LANES, cv), prv(nlo, clo),
        par(LANES, cr), par(LANES, ck), par(LANES, cv), par(nlo, clo),
    ] + [par(LANES, cr)] * 7 + [
        pl.BlockSpec((LANES, LANES), lambda b, p, t: (0, p)),
        pl.BlockSpec((LANES, LANES), lambda b, p, t: (0, p)),
        pl.BlockSpec((LORA_G, LANES), lambda b, p, t: (0, p)),
    ]
    return pl.pallas_call(
        functools.partial(_rwkv_kernel, tb=tb),
        out_shape=jax.ShapeDtypeStruct((bsz, s, w), BF16),
        grid=(bsz, npair, s // tb),
        in_specs=in_specs,
        out_specs=pl.BlockSpec((None, tb, LANES), lambda b, p, t: (b, t, p)),
        scratch_shapes=[pltpu.VMEM((LANES, LANES), F32)],
        compiler_params=_cparams(("parallel", "parallel", "arbitrary")),
        name="rwkv7",
    )(feat, feat, feat, lora, feat, feat, feat, lora,
      vec(mu), vec(mu), vec(mu), vec(mu_lora),
      vec(w0), vec(a0), vec(k_k), vec(k_a), vec(r_k), vec(gn_w), vec(gn_b),
      w2p, a2p, g2)


def _pad_cols(x, n):
    return jnp.pad(x, ((0, 0), (0, n - x.shape[1])))


def _pad_rows(x, n):
    return jnp.pad(x, ((0, n - x.shape[0]), (0, 0)))


def _rwkv_lora_cols(x):
    c0 = 3 * RWKV_WIDTH
    cw = x[..., c0:c0 + LORA_W]
    ca = x[..., c0 + LORA_W:c0 + LORA_W + LORA_A]
    cg = x[..., c0 + LORA_W + LORA_A:]
    pad = lambda y: jnp.pad(y, [(0, 0)] * (y.ndim - 1) + [(0, LANES - y.shape[-1])])
    return jnp.concatenate([pad(cw), pad(ca), cg], axis=-1)


ROUTER_COLS = LANES


def _router_params(rg, rgb, re, reb):
    d = rg.shape[0]
    w = jnp.zeros((d, ROUTER_COLS), F32).at[:, :N_GROUPS].set(rg)
    w = w.at[:, N_GROUPS:N_GROUPS + N_EXPERTS].set(re)
    b = jnp.zeros((ROUTER_COLS,), F32).at[:N_GROUPS].set(rgb)
    b = b.at[N_GROUPS:N_GROUPS + N_EXPERTS].set(reb)
    return w, b


def _tail(h_f32, mix_parts, w_out, ln1_g, ln1_b, rg, rgb, re, reb, wg, wu, wd, ln2_g, ln2_b):
    mixed = proj(mix_parts, w_out, out_dtype=BF16)
    rw, rb = _router_params(rg, rgb, re, reb)
    h1_f32, logits = ln_router(h_f32, mixed, ln1_g, ln1_b, rw, rb)
    return hier_moe_ln(h1_f32, logits, wg, wu, wd, ln2_g, ln2_b)


def kernel(x, l0_w_in, l0_shift_mu, l0_w0, l0_w2, l0_a0, l0_a2, l0_g2, l0_k_k, l0_k_a, l0_r_k, l0_gn_w, l0_gn_b, l0_w_out, l0_ln1_g, l0_ln1_b, l0_router_g, l0_router_g_b, l0_router_e, l0_router_e_b, l0_w_gate, l0_w_up, l0_w_down, l0_ln2_g, l0_ln2_b, l1_w_in, l1_lq1, l1_lk1, l1_lq2, l1_lk2, l1_subln_w, l1_gate_w2, l1_gate_b, l1_gla_norm_w, l1_w_out, l1_ln1_g, l1_ln1_b, l1_router_g, l1_router_g_b, l1_router_e, l1_router_e_b, l1_w_gate, l1_w_up, l1_w_down, l1_ln2_g, l1_ln2_b):
    bsz, s, d = x.shape
    t = bsz * s
    h0 = x.reshape(t, d)

    sbw = 3 * SB_WIDTH
    rww = 3 * RWKV_WIDTH
    x_bf16 = h0.astype(BF16)
    qkv0 = proj([x_bf16], l0_w_in, col0=0, ncols=sbw, out_dtype=BF16)
    feat = proj([x_bf16], l0_w_in, col0=sbw, ncols=rww, out_dtype=F32)
    lora = proj([x_bf16], _rwkv_lora_cols(l0_w_in[:, sbw:]), out_dtype=F32)
    o_sb = sb_attention(qkv0.reshape(bsz, s, sbw))
    o_rw = rwkv7_mixer(feat.reshape(bsz, s, rww), lora.reshape(bsz, s, RWKV_LORA_COLS),
                       l0_shift_mu[:rww], _rwkv_lora_cols(l0_shift_mu[None])[0],
                       l0_w0, l0_a0, l0_k_k, l0_k_a, l0_r_k.reshape(-1), l0_gn_w, l0_gn_b,
                       _pad_rows(l0_w2, LANES), _pad_rows(l0_a2, LANES), l0_g2)
    h2_f32, h2_bf16 = _tail(h0, [o_sb.reshape(t, SB_WIDTH), o_rw.reshape(t, RWKV_WIDTH)],
                            l0_w_out, l0_ln1_g, l0_ln1_b, l0_router_g, l0_router_g_b,
                            l0_router_e, l0_router_e_b, l0_w_gate, l0_w_up, l0_w_down,
                            l0_ln2_g, l0_ln2_b)

    dfw = 3 * DIFF_WIDTH
    glw = 2 * GLA_QK + GLA_WIDTH
    g0 = dfw + glw
    g1 = g0 + GLA_GATE_RANK
    qkv1 = proj([h2_bf16], l1_w_in, col0=0, ncols=dfw, out_dtype=BF16)
    gproj = proj([h2_bf16], l1_w_in, col0=dfw, ncols=glw, out_dtype=F32)
    og = proj([h2_bf16], l1_w_in[:, g1:], out_dtype=F32)
    cgp = proj_exact(h2_f32, _pad_cols(l1_w_in[:, g0:g1], LANES))
    lam_init = 0.8 - 0.6 * math.exp(-0.3 * 1)
    lam = (jnp.exp(jnp.sum(l1_lq1 * l1_lk1)) - jnp.exp(jnp.sum(l1_lq2 * l1_lk2)) + lam_init)
    o_diff = diff_attention(qkv1.reshape(bsz, s, dfw), lam, l1_subln_w, lam_init)
    o_gla = gla_mixer(gproj.reshape(bsz, s, glw), og.reshape(bsz, s, GLA_WIDTH),
                      cgp.reshape(bsz, s, LANES), _pad_rows(l1_gate_w2, LANES), l1_gate_b,
                      l1_gla_norm_w)
    out_f32, _ = _tail(h2_f32, [o_diff.reshape(t, DIFF_WIDTH), o_gla.reshape(t, GLA_WIDTH)],
                       l1_w_out, l1_ln1_g, l1_ln1_b, l1_router_g, l1_router_g_b,
                       l1_router_e, l1_router_e_b, l1_w_gate, l1_w_up, l1_w_down,
                       l1_ln2_g, l1_ln2_b)
    return out_f32.reshape(bsz, s, d)
```

```python
import functools
import math

import jax
import jax.numpy as jnp
from jax import lax
from jax.experimental import pallas as pl
from jax.experimental.pallas import tpu as pltpu

F32 = jnp.float32
BF16 = jnp.bfloat16
I32 = jnp.int32

D_MODEL = 4096
DEPTH = 2
CHUNK = 64
SB_WIDTH = 2048
SB_HEAD_DIM = 128
SB_HEADS = SB_WIDTH // SB_HEAD_DIM
RWKV_WIDTH = 2048
RWKV_HEAD_DIM = 64
LORA_W = 96
LORA_A = 96
LORA_G = 256
RWKV_DECAY_SCALE = math.exp(-0.5)
RWKV_GN_EPS = 64e-5
DIFF_WIDTH = 2048
DIFF_HEAD_DIM = 128
DIFF_HEADS = DIFF_WIDTH // (2 * DIFF_HEAD_DIM)
GLA_WIDTH = 2048
GLA_HEADS = 4
GLA_QK = GLA_WIDTH // 2
GLA_DK = GLA_QK // GLA_HEADS
GLA_DV = GLA_WIDTH // GLA_HEADS
GLA_GATE_RANK = 16
GLA_GATE_NORM = 16.0
N_GROUPS = 4
EXPERTS_PER_GROUP = 8
N_EXPERTS = N_GROUPS * EXPERTS_PER_GROUP
TOP_K = 2
D_EXPERT = 768
DEEPNORM_ALPHA = (2 * DEPTH) ** 0.25
LN_EPS = 1e-5

LANES = 128
V7X_VMEM_LIMIT = 56 * 1024 * 1024

def _cparams(semantics, vmem=V7X_VMEM_LIMIT):
    return pltpu.CompilerParams(dimension_semantics=semantics, vmem_limit_bytes=vmem)


def _dot(a, b, dims=(((1,), (0,)), ((), ()))):
    return lax.dot_general(a.astype(BF16), b.astype(BF16), dims, preferred_element_type=F32)


def _split2(x):
    hi = x.astype(BF16)
    return hi, (x - hi.astype(F32)).astype(BF16)


def _dot_split(a, b, *, exact, dims=(((1,), (0,)), ((), ()))):
    dn = lambda x, y: lax.dot_general(x, y, dims, preferred_element_type=F32)
    if exact == 'a':
        hi, lo = _split2(b)
        a16 = a.astype(BF16)
        return dn(a16, hi) + dn(a16, lo)
    hi, lo = _split2(a)
    b16 = b.astype(BF16)
    return dn(hi, b16) + dn(lo, b16)


def _dot3(a, b):
    ah, al = _split2(a)
    bh, bl = _split2(b)
    dn = lambda x, y: lax.dot_general(x, y, (((1,), (0,)), ((), ())), preferred_element_type=F32)
    return dn(ah, bh) + dn(ah, bl) + dn(al, bh)


_NT = (((1,), (1,)), ((), ()))
_TN = (((0,), (0,)), ((), ()))


PROJ_TM = 1024
PROJ_TN = 512


def _proj_kernel(*refs, nx):
    x_refs, w_ref, o_ref, wb_ref = refs[:nx], refs[nx], refs[nx + 1], refs[nx + 2]

    @pl.when(pl.program_id(1) == 0)
    def _():
        wb_ref[...] = w_ref[...].astype(BF16)

    acc = None
    k0 = 0
    for x_ref in x_refs:
        kx = x_ref.shape[1]
        part = lax.dot_general(x_ref[...], wb_ref[k0:k0 + kx, :], (((1,), (0,)), ((), ())),
                               preferred_element_type=F32)
        acc = part if acc is None else acc + part
        k0 += kx
    o_ref[...] = acc.astype(o_ref.dtype)


def proj(xs, w, *, col0=0, ncols=None, out_dtype):
    m = xs[0].shape[0]
    k = w.shape[0]
    ncols = w.shape[1] - col0 if ncols is None else ncols
    tm, tn = PROJ_TM, PROJ_TN
    assert sum(x.shape[1] for x in xs) == k and m % tm == 0
    assert ncols % tn == 0 and col0 % tn == 0, (col0, ncols)
    cb = col0 // tn
    in_specs = [pl.BlockSpec((tm, x.shape[1]), lambda j, i: (i, 0)) for x in xs]
    in_specs.append(pl.BlockSpec((k, tn), lambda j, i: (0, cb + j)))
    return pl.pallas_call(
        functools.partial(_proj_kernel, nx=len(xs)),
        out_shape=jax.ShapeDtypeStruct((m, ncols), out_dtype),
        grid=(ncols // tn, m // tm),
        in_specs=in_specs,
        out_specs=pl.BlockSpec((tm, tn), lambda j, i: (i, j)),
        scratch_shapes=[pltpu.VMEM((k, tn), BF16)],
        compiler_params=_cparams(("arbitrary", "arbitrary")),
        name="proj",
    )(*xs, w)


def _proj_exact_kernel(x_ref, w_ref, o_ref):
    o_ref[...] = _dot3(x_ref[...], w_ref[...])


def proj_exact(x, w, *, tm=512):
    m, k = x.shape
    n = w.shape[1]
    return pl.pallas_call(
        _proj_exact_kernel,
        out_shape=jax.ShapeDtypeStruct((m, n), F32),
        grid=(m // tm,),
        in_specs=[pl.BlockSpec((tm, k), lambda i: (i, 0)), pl.BlockSpec((k, n), lambda i: (0, 0))],
        out_specs=pl.BlockSpec((tm, n), lambda i: (i, 0)),
        compiler_params=_cparams(("parallel",)),
        name="proj_exact",
    )(x, w)


def _layer_norm_rows(x, g, b):
    mu = jnp.mean(x, axis=-1, keepdims=True)
    xc = x - mu
    var = jnp.mean(xc * xc, axis=-1, keepdims=True)
    return xc * lax.rsqrt(var + LN_EPS) * g + b


def _ln_router_kernel(h_ref, m_ref, g_ref, b_ref, r_ref, rb_ref, of_ref, lg_ref):
    y = _layer_norm_rows(DEEPNORM_ALPHA * h_ref[...] + m_ref[...].astype(F32),
                         g_ref[...], b_ref[...])
    of_ref[...] = y
    lg_ref[...] = _dot3(y, r_ref[...]) + rb_ref[...]


def ln_router(h, mixed, g, b, router_w, router_b, *, tm=256):
    t, d = h.shape
    nr = router_w.shape[1]
    row = lambda i: (i, 0)
    fixed = lambda i: (0, 0)
    return pl.pallas_call(
        _ln_router_kernel,
        out_shape=(jax.ShapeDtypeStruct((t, d), F32), jax.ShapeDtypeStruct((t, nr), F32)),
        grid=(t // tm,),
        in_specs=[pl.BlockSpec((tm, d), row), pl.BlockSpec((tm, d), row),
                  pl.BlockSpec((1, d), fixed), pl.BlockSpec((1, d), fixed),
                  pl.BlockSpec((d, nr), fixed), pl.BlockSpec((1, nr), fixed)],
        out_specs=(pl.BlockSpec((tm, d), row), pl.BlockSpec((tm, nr), row)),
        compiler_params=_cparams(("parallel",)),
        name="ln_router",
    )(h, mixed, g.reshape(1, d), b.reshape(1, d), router_w, router_b.reshape(1, nr))


MOE_ROWS = 256
MOE_UP_SPLIT = 2
MOE_TM = 256
U32 = jnp.uint32


def _pack_halves(y):
    n = y.shape[1] // 2
    bits = pltpu.bitcast(y, U32)
    rounded = bits + U32(0x7FFF) + ((bits >> 16) & U32(1))
    return (rounded[:, :n] & U32(0xFFFF0000)) | (rounded[:, n:] >> 16)


def _unpack_halves_f32(u):
    return pltpu.bitcast(u & U32(0xFFFF0000), F32), pltpu.bitcast(u << 16, F32)


def _unpack_halves(u):
    hi, lo = _unpack_halves_f32(u)
    return hi.astype(BF16), lo.astype(BF16)


DMA_UNROLL = 8

ET_NEXT, ET_SLOT, ET_LAST, ET_HAS, ET_NB = 0, 1, 2, 3, 4


def _dispatch_kernel(pos_ref, et_ref, h_ref, xs_out, buf, zbuf, sem, zsem, *, tm, nsteps):
    i = pl.program_id(0)
    slot = i % 2

    @pl.when(i == 0)
    def _():
        zbuf[...] = jnp.zeros(zbuf.shape, zbuf.dtype)

        def zero_copy(e):
            first = pl.multiple_of(et_ref[ET_LAST, e], MOE_ROWS)
            return pltpu.make_async_copy(zbuf, xs_out.at[pl.ds(first, MOE_ROWS), :], zsem)

        def unused_copy(blk):
            first = pl.multiple_of(blk * MOE_ROWS, MOE_ROWS)
            return pltpu.make_async_copy(zbuf, xs_out.at[pl.ds(first, MOE_ROWS), :], zsem)

        nblk = xs_out.shape[0] // MOE_ROWS
        nb = et_ref[ET_NB, 0]
        for e in range(N_EXPERTS):
            @pl.when(et_ref[ET_HAS, e] > 0)
            def _():
                zero_copy(e).start()
        lax.fori_loop(nb, nblk, lambda blk, c: (unused_copy(blk).start(), c)[1], 0)
        for e in range(N_EXPERTS):
            @pl.when(et_ref[ET_HAS, e] > 0)
            def _():
                zero_copy(e).wait()
        lax.fori_loop(nb, nblk, lambda blk, c: (unused_copy(blk).wait(), c)[1], 0)

    def row_copy(s, r, p):
        return pltpu.make_async_copy(buf.at[s, pl.ds(r, 1), :], xs_out.at[pl.ds(p, 1), :],
                                     sem.at[s])

    def drain(s):
        def body(r, c):
            for _ in range(TOP_K):
                row_copy(s, r, 0).wait()
            return c
        lax.fori_loop(0, tm, body, 0, unroll=DMA_UNROLL)

    @pl.when(i >= 2)
    def _():
        drain(slot)

    buf[slot] = _pack_halves(h_ref[...])

    def issue(r, c):
        a = (i * tm + r) * TOP_K
        for k in range(TOP_K):
            row_copy(slot, r, pos_ref[a + k]).start()
        return c
    lax.fori_loop(0, tm, issue, 0, unroll=DMA_UNROLL)

    @pl.when(i == nsteps - 1)
    def _():
        drain(slot)
        if nsteps > 1:
            drain(1 - slot)


def moe_dispatch(h, pos, etab, nslots):
    t, d = h.shape
    tm = MOE_TM
    nsteps = t // tm
    grid_spec = pltpu.PrefetchScalarGridSpec(
        num_scalar_prefetch=2,
        grid=(nsteps,),
        in_specs=[pl.BlockSpec((tm, d), lambda i, pos, et: (i, 0))],
        out_specs=pl.BlockSpec(memory_space=pl.ANY),
        scratch_shapes=[pltpu.VMEM((2, tm, d // 2), U32), pltpu.VMEM((MOE_ROWS, d // 2), U32),
                        pltpu.SemaphoreType.DMA((2,)), pltpu.SemaphoreType.DMA(())],
    )
    return pl.pallas_call(
        functools.partial(_dispatch_kernel, tm=tm, nsteps=nsteps),
        out_shape=jax.ShapeDtypeStruct((nslots, d // 2), U32),
        grid_spec=grid_spec,
        compiler_params=_cparams(("arbitrary",)),
        name="moe_dispatch",
    )(pos.reshape(-1), etab, h)


def _expert_changed(be_ref, i):
    return (i == 0) | (be_ref[i] != be_ref[jnp.maximum(i - 1, 0)])


def _expert_weights(be_ref, et_ref, i, copies, convert):
    e = be_ref[i]
    slot = et_ref[ET_SLOT, e]

    @pl.when(_expert_changed(be_ref, i))
    def _():
        @pl.when(i == 0)
        def _():
            for c in copies(e, slot):
                c.start()
        for c in copies(e, slot):
            c.wait()
        nxt = et_ref[ET_NEXT, e]

        @pl.when(nxt >= 0)
        def _():
            for c in copies(nxt, 1 - slot):
                c.start()
        convert(slot)


def _moe_up_kernel(be_ref, nb_ref, et_ref, xs_ref, wg_hbm, wu_hbm, a_ref, wst, wc_ref, sem):
    j = pl.program_id(0)
    i = pl.program_id(1)
    half = xs_ref.shape[1]
    ce = a_ref.shape[1]

    def copies(e, slot):
        cols = pl.ds(pl.multiple_of(j * ce, LANES), ce)
        return [pltpu.make_async_copy(wg_hbm.at[e, :, cols], wst.at[slot, 0], sem.at[slot]),
                pltpu.make_async_copy(wu_hbm.at[e, :, cols], wst.at[slot, 1], sem.at[slot])]

    def convert(slot):
        for hh in range(2):
            rows = slice(hh * half, (hh + 1) * half)
            wc_ref[hh] = jnp.concatenate([wst[slot, 0, rows, :], wst[slot, 1, rows, :]],
                                         axis=1).astype(BF16)

    @pl.when(i < nb_ref[0])
    def _():
        _expert_weights(be_ref, et_ref, i, copies, convert)
        hi, lo = _unpack_halves(xs_ref[...])
        hgu = _dot(hi, wc_ref[0]) + _dot(lo, wc_ref[1])
        hg, hu = hgu[:, :ce], hgu[:, ce:]
        a_ref[...] = (hg * jax.nn.sigmoid(hg) * hu).astype(a_ref.dtype)

    @pl.when(i >= nb_ref[0])
    def _():
        a_ref[...] = jnp.zeros_like(a_ref)


def moe_up(xs, block_e, nb_used, etab, wg, wu):
    nslots, half = xs.shape
    _, d, de = wg.shape
    rows = MOE_ROWS
    nblk = nslots // rows
    ce = de // MOE_UP_SPLIT
    last = lambda i, nb: jnp.minimum(i, nb[0] - 1)
    grid_spec = pltpu.PrefetchScalarGridSpec(
        num_scalar_prefetch=3,
        grid=(MOE_UP_SPLIT, nblk),
        in_specs=[
            pl.BlockSpec((rows, half), lambda j, i, be, nb, et: (last(i, nb), 0)),
            pl.BlockSpec(memory_space=pl.ANY),
            pl.BlockSpec(memory_space=pl.ANY),
        ],
        out_specs=pl.BlockSpec((rows, ce), lambda j, i, be, nb, et: (i, j)),
        scratch_shapes=[pltpu.VMEM((2, 2, d, ce), F32), pltpu.VMEM((2, half, 2 * ce), BF16),
                        pltpu.SemaphoreType.DMA((2,))],
    )
    return pl.pallas_call(
        _moe_up_kernel,
        out_shape=jax.ShapeDtypeStruct((nslots, de), BF16),
        grid_spec=grid_spec,
        compiler_params=_cparams(("arbitrary", "arbitrary")),
        name="moe_up",
    )(block_e, nb_used, etab, xs, wg, wu)


def _moe_down_kernel(be_ref, nb_ref, et_ref, a_ref, wd_hbm, o_ref, wst, wc_ref, sem):
    i = pl.program_id(0)

    def copies(e, slot):
        return [pltpu.make_async_copy(wd_hbm.at[e], wst.at[slot], sem.at[slot])]

    def convert(slot):
        wc_ref[...] = wst[slot].astype(BF16)

    @pl.when(i < nb_ref[0])
    def _():
        _expert_weights(be_ref, et_ref, i, copies, convert)
        o_ref[...] = _pack_halves(_dot(a_ref[...], wc_ref[...]))

    @pl.when(i >= nb_ref[0])
    def _():
        o_ref[...] = jnp.zeros_like(o_ref)


def moe_down(act, block_e, nb_used, etab, wd):
    nslots, de = act.shape
    d = wd.shape[2]
    rows = MOE_ROWS
    nblk = nslots // rows
    last = lambda i, nb: jnp.minimum(i, nb[0] - 1)
    grid_spec = pltpu.PrefetchScalarGridSpec(
        num_scalar_prefetch=3,
        grid=(nblk,),
        in_specs=[
            pl.BlockSpec((rows, de), lambda i, be, nb, et: (last(i, nb), 0)),
            pl.BlockSpec(memory_space=pl.ANY),
        ],
        out_specs=pl.BlockSpec((rows, d // 2), lambda i, be, nb, et: (i, 0)),
        scratch_shapes=[pltpu.VMEM((2, de, d), F32), pltpu.VMEM((de, d), BF16),
                        pltpu.SemaphoreType.DMA((2,))],
    )
    return pl.pallas_call(
        _moe_down_kernel,
        out_shape=jax.ShapeDtypeStruct((nslots, d // 2), U32),
        grid_spec=grid_spec,
        compiler_params=_cparams(("arbitrary",)),
        name="moe_down",
    )(block_e, nb_used, etab, act, wd)


def _combine_ln_kernel(pos_ref, h_ref, gate_ref, yb_hbm, g_ref, b_ref, of_ref, ob_ref,
                       ybuf, sem, *, tm, nsteps):
    i = pl.program_id(0)
    slot = i % 2

    def row_copy(s, k, r, p):
        return pltpu.make_async_copy(yb_hbm.at[pl.ds(p, 1), :], ybuf.at[s, k, pl.ds(r, 1), :],
                                     sem.at[s])

    def issue(step, s):
        def body(r, c):
            a = (step * tm + r) * TOP_K
            for k in range(TOP_K):
                row_copy(s, k, r, pos_ref[a + k]).start()
            return c
        lax.fori_loop(0, tm, body, 0, unroll=DMA_UNROLL)

    @pl.when(i == 0)
    def _():
        issue(0, 0)

    @pl.when(i + 1 < nsteps)
    def _():
        issue(i + 1, 1 - slot)

    def drain(r, c):
        for k in range(TOP_K):
            row_copy(slot, k, r, 0).wait()
        return c
    lax.fori_loop(0, tm, drain, 0, unroll=DMA_UNROLL)

    gate = gate_ref[...]
    y_hi = y_lo = None
    for k in range(TOP_K):
        hi, lo = _unpack_halves_f32(ybuf[slot, k])
        y_hi = hi * gate[:, k:k + 1] if y_hi is None else y_hi + hi * gate[:, k:k + 1]
        y_lo = lo * gate[:, k:k + 1] if y_lo is None else y_lo + lo * gate[:, k:k + 1]
    y = jnp.concatenate([y_hi, y_lo], axis=1)
    out = _layer_norm_rows(DEEPNORM_ALPHA * h_ref[...] + y, g_ref[...], b_ref[...])
    of_ref[...] = out
    ob_ref[...] = out.astype(BF16)


def combine_ln(h, gates, pos, yb, g, b):
    t, d = h.shape
    tm = MOE_TM
    nsteps = t // tm
    grid_spec = pltpu.PrefetchScalarGridSpec(
        num_scalar_prefetch=1,
        grid=(nsteps,),
        in_specs=[
            pl.BlockSpec((tm, d), lambda i, pos: (i, 0)),
            pl.BlockSpec((tm, TOP_K), lambda i, pos: (i, 0)),
            pl.BlockSpec(memory_space=pl.ANY),
            pl.BlockSpec((1, d), lambda i, pos: (0, 0)),
            pl.BlockSpec((1, d), lambda i, pos: (0, 0)),
        ],
        out_specs=(pl.BlockSpec((tm, d), lambda i, pos: (i, 0)),
                   pl.BlockSpec((tm, d), lambda i, pos: (i, 0))),
        scratch_shapes=[pltpu.VMEM((2, TOP_K, tm, d // 2), U32), pltpu.SemaphoreType.DMA((2,))],
    )
    return pl.pallas_call(
        functools.partial(_combine_ln_kernel, tm=tm, nsteps=nsteps),
        out_shape=(jax.ShapeDtypeStruct((t, d), F32), jax.ShapeDtypeStruct((t, d), BF16)),
        grid_spec=grid_spec,
        compiler_params=_cparams(("arbitrary",)),
        name="combine_ln",
    )(pos.reshape(-1), h, gates, yb, g.reshape(1, d), b.reshape(1, d))


def _routing_tables(logits, t):
    g_logits = logits[:, :N_GROUPS]
    g_prob = jax.nn.softmax(g_logits, axis=-1)
    g_sel = jnp.argmax(g_logits, axis=-1)
    g_w = jnp.take_along_axis(g_prob, g_sel[:, None], axis=-1)[:, 0]
    e_logits = logits[:, N_GROUPS:N_GROUPS + N_EXPERTS].reshape(t, N_GROUPS, EXPERTS_PER_GROUP)
    e_in_group = jnp.take_along_axis(e_logits, g_sel[:, None, None], axis=1)[:, 0]
    top_vals, top_idx = lax.top_k(e_in_group, TOP_K)
    gates = jax.nn.softmax(top_vals, axis=-1) * g_w[:, None]
    expert_id = (g_sel[:, None] * EXPERTS_PER_GROUP + top_idx).astype(I32)

    a = t * TOP_K
    rows = MOE_ROWS
    flat_e = expert_id.reshape(a)
    onehot = (flat_e[:, None] == jnp.arange(N_EXPERTS, dtype=I32)[None, :]).astype(I32)
    seen = jnp.cumsum(onehot, axis=0)
    counts = seen[-1]
    padded = (counts + rows - 1) // rows * rows
    pad_end = jnp.cumsum(padded)
    pad_start = pad_end - padded
    pos = jnp.sum(onehot * (seen - 1 + pad_start[None, :]), axis=1).astype(I32)
    nblk = a // rows + N_EXPERTS
    block_start = jnp.arange(nblk, dtype=I32) * rows
    block_e = jnp.minimum(jnp.sum((pad_end[None, :] <= block_start[:, None]).astype(I32), axis=1),
                          N_EXPERTS - 1).astype(I32)
    nb_used = (pad_end[-1] // rows).astype(I32).reshape(1)
    eidx = jnp.arange(N_EXPERTS, dtype=I32)
    has = counts > 0
    from_here = lax.cummin(jnp.where(has, eidx, N_EXPERTS), reverse=True)
    nxt = jnp.concatenate([from_here[1:], jnp.full((1,), N_EXPERTS, I32)])
    nxt = jnp.where(nxt >= N_EXPERTS, -1, nxt)
    run_slot = (jnp.cumsum(has.astype(I32)) - has.astype(I32)) % 2
    last_first = jnp.where(has, pad_end - rows, 0)
    etab = jnp.stack([nxt, run_slot, last_first, has.astype(I32),
                      jnp.broadcast_to(nb_used, (N_EXPERTS,))]).astype(I32)
    return gates.astype(F32), pos.reshape(t, TOP_K), block_e, nb_used, etab, nblk * rows


def hier_moe_ln(h_f32, logits, wg, wu, wd, ln_g, ln_b):
    t = h_f32.shape[0]
    gates, pos, block_e, nb_used, etab, nslots = _routing_tables(logits, t)
    xs = moe_dispatch(h_f32, pos, etab, nslots)
    act = moe_up(xs, block_e, nb_used, etab, wg, wu)
    yb = moe_down(act, block_e, nb_used, etab, wd)
    return combine_ln(h_f32, gates, pos, yb, ln_g, ln_b)


DIFF_T = 512
NEG_BIG = -1e30


def _lane_tile(x, n):
    return x if n == 1 else jnp.concatenate([x] * n, axis=1)


DIFF_HEADS_PER_STEP = 2


def _diff_attn_kernel(lam_ref, q_ref, k_ref, v_ref, sw_ref, o_ref,
                      m_ref, l_ref, acc_ref, *, t, nhs, scale, out_scale):
    i = pl.program_id(2)
    d = DIFF_HEAD_DIM
    chains = range(2 * nhs)
    qk_cols = [slice(c * d, (c + 1) * d) for c in chains]
    v_cols = [slice(c // 2 * 2 * d, (c // 2 + 1) * 2 * d) for c in chains]
    qs = [(q_ref[:, qk_cols[c]].astype(F32) * scale).astype(BF16) for c in chains]
    m_ref[...] = jnp.full(m_ref.shape, NEG_BIG, F32)
    l_ref[...] = jnp.zeros(l_ref.shape, F32)
    acc_ref[...] = jnp.zeros(acc_ref.shape, F32)
    krep = t // LANES
    vrep = 2 * d // LANES

    def block(j, diagonal):
        start = pl.multiple_of(j * t, t)
        s = [lax.dot_general(qs[c], k_ref[pl.ds(start, t), qk_cols[c]], _NT,
                             preferred_element_type=F32) for c in chains]
        if diagonal:
            row_chunk = lax.broadcasted_iota(I32, (t, t), 0) // CHUNK
            col_chunk = lax.broadcasted_iota(I32, (t, t), 1) // CHUNK
            s = [jnp.where(col_chunk <= row_chunk, s[c], NEG_BIG) for c in chains]
        m_prev = [m_ref[c] for c in chains]
        m_new = [jnp.maximum(m_prev[c], jnp.max(s[c], axis=1, keepdims=True)) for c in chains]
        p = [jnp.exp(s[c] - _lane_tile(m_new[c], krep)) for c in chains]
        alpha = [jnp.exp(m_prev[c] - m_new[c]) for c in chains]
        l_new = [alpha[c] * l_ref[c] + jnp.sum(p[c], axis=1, keepdims=True) for c in chains]
        pv = [_dot(p[c], v_ref[pl.ds(start, t), v_cols[c]]) for c in chains]
        for c in chains:
            acc_ref[c] = _lane_tile(alpha[c], vrep) * acc_ref[c] + pv[c]
            m_ref[c] = m_new[c]
            l_ref[c] = l_new[c]

    def full_block(j, carry):
        block(j, False)
        return carry

    lax.fori_loop(0, i, full_block, 0)
    block(i, True)
    lam = lam_ref[0]
    for h in range(nhs):
        inv = [_lane_tile(1.0 / l_ref[2 * h + b], vrep) for b in range(2)]
        o = acc_ref[2 * h] * inv[0] - lam * (acc_ref[2 * h + 1] * inv[1])
        ms = jnp.mean(o * o, axis=-1, keepdims=True)
        o_ref[:, v_cols[2 * h]] = (o * lax.rsqrt(ms + 1e-5) * sw_ref[...]
                                   * out_scale).astype(o_ref.dtype)


def diff_attention(qkv, lam, subln_w, lam_init):
    bsz, s, _ = qkv.shape
    tq = min(DIFF_T, s)
    assert s % tq == 0 and tq % CHUNK == 0
    d = DIFF_HEAD_DIM
    nhs = DIFF_HEADS_PER_STEP
    w = nhs * 2 * d
    ng = DIFF_WIDTH // w
    grid_spec = pltpu.PrefetchScalarGridSpec(
        num_scalar_prefetch=0,
        grid=(bsz, ng, s // tq),
        in_specs=[
            pl.BlockSpec(memory_space=pltpu.SMEM),
            pl.BlockSpec((None, tq, w), lambda b, g, i: (b, i, g)),
            pl.BlockSpec((None, s, w), lambda b, g, i: (b, 0, ng + g)),
            pl.BlockSpec((None, s, w), lambda b, g, i: (b, 0, 2 * ng + g)),
            pl.BlockSpec((1, 2 * d), lambda b, g, i: (0, 0)),
        ],
        out_specs=pl.BlockSpec((None, tq, w), lambda b, g, i: (b, i, g)),
        scratch_shapes=[pltpu.VMEM((2 * nhs, tq, LANES), F32),
                        pltpu.VMEM((2 * nhs, tq, LANES), F32),
                        pltpu.VMEM((2 * nhs, tq, 2 * d), F32)],
    )
    return pl.pallas_call(
        functools.partial(_diff_attn_kernel, t=tq, nhs=nhs, scale=d ** -0.5,
                          out_scale=1.0 - lam_init),
        out_shape=jax.ShapeDtypeStruct((bsz, s, DIFF_WIDTH), BF16),
        grid_spec=grid_spec,
        compiler_params=_cparams(("parallel", "parallel", "arbitrary")),
        name="diff_attn",
    )(lam.reshape(1).astype(F32), qkv, qkv, qkv, subln_w.reshape(1, 2 * d))


GLA_TB = 1024


def _log_sigmoid(x):
    return jnp.minimum(x, 0.0) - jnp.log(1.0 + jnp.exp(-jnp.abs(x)))


def _gla_kernel(q_ref, k_ref, v_ref, og_ref, cg_ref, gw_ref, gb_ref, nw_ref, o_ref,
                state_ref, *, tb, scale):
    t = pl.program_id(2)

    @pl.when(t == 0)
    def _():
        state_ref[...] = jnp.zeros(state_ref.shape, F32)

    c = CHUNK
    ri = lax.broadcasted_iota(I32, (c, c), 0)
    ci = lax.broadcasted_iota(I32, (c, c), 1)
    causal = ci <= ri
    ltri = causal.astype(F32)
    ones_cols = jnp.ones((c, LANES), F32)
    dv = v_ref.shape[-1]
    nch = range(tb // c)
    rows = [slice(n * c, (n + 1) * c) for n in nch]
    gl = [_dot3(cg_ref[rows[n], :], gw_ref[...]) + gb_ref[...] for n in nch]
    log_a = [_log_sigmoid(gl[n]) * (1.0 / GLA_GATE_NORM) for n in nch]
    bcum = [_dot_split(ltri, log_a[n], exact='a') for n in nch]
    v = [v_ref[rows[n], :] for n in nch]
    q_dec = [q_ref[rows[n], :] * scale * jnp.exp(bcum[n]) for n in nch]
    k_inv = [k_ref[rows[n], :] * jnp.exp(-bcum[n]) for n in nch]
    k_end = [k_ref[rows[n], :] * jnp.exp(bcum[n][c - 1:c, :] - bcum[n]) for n in nch]
    scores = [jnp.where(causal, _dot(q_dec[n], k_inv[n], _NT), 0.0) for n in nch]
    o_intra = [_dot(scores[n], v[n]) for n in nch]
    dec = [_lane_tile(jnp.exp(_dot_split(log_a[n], ones_cols, exact='b', dims=_TN)),
                      dv // LANES) for n in nch]
    kv = [_dot(k_end[n].T, v[n]) for n in nch]

    state = state_ref[...]
    o = []
    for n in nch:
        o.append(o_intra[n] + _dot(q_dec[n], state))
        state = state * dec[n] + kv[n]
    state_ref[...] = state

    for n in nch:
        ms = jnp.mean(o[n] * o[n], axis=-1, keepdims=True)
        og = og_ref[rows[n], :]
        gate = og * jax.nn.sigmoid(og)
        o_ref[rows[n], :] = (o[n] * lax.rsqrt(ms + 1e-5) * nw_ref[...] * gate).astype(o_ref.dtype)


def gla_mixer(proj, og, cg, gate_w2p, gate_b, norm_w):
    bsz, s, _ = proj.shape
    tb = min(GLA_TB, s)
    assert s % tb == 0 and tb % CHUNK == 0
    dk, dv, nh = GLA_DK, GLA_DV, GLA_HEADS
    in_specs = [
        pl.BlockSpec((None, tb, dk), lambda b, h, t: (b, t, h)),
        pl.BlockSpec((None, tb, dk), lambda b, h, t: (b, t, nh + h)),
        pl.BlockSpec((None, tb, dv), lambda b, h, t: (b, t, nh + h)),
        pl.BlockSpec((None, tb, dv), lambda b, h, t: (b, t, h)),
        pl.BlockSpec((None, tb, LANES), lambda b, h, t: (b, t, 0)),
        pl.BlockSpec((LANES, dk), lambda b, h, t: (0, h)),
        pl.BlockSpec((1, dk), lambda b, h, t: (0, h)),
        pl.BlockSpec((1, dv), lambda b, h, t: (0, 0)),
    ]
    return pl.pallas_call(
        functools.partial(_gla_kernel, tb=tb, scale=dk ** -0.5),
        out_shape=jax.ShapeDtypeStruct((bsz, s, GLA_WIDTH), BF16),
        grid=(bsz, nh, s // tb),
        in_specs=in_specs,
        out_specs=pl.BlockSpec((None, tb, dv), lambda b, h, t: (b, t, h)),
        scratch_shapes=[pltpu.VMEM((dk, dv), F32)],
        compiler_params=_cparams(("parallel", "parallel", "arbitrary")),
        name="gla",
    )(proj, proj, proj, og, cg, gate_w2p, gate_b.reshape(1, GLA_QK), norm_w.reshape(1, dv))


SB_T = 256
SB_HEADS_PER_STEP = 4
SB_UNDERFLOW = 110.0


def _sb_attn_kernel(q_ref, k_ref, v_ref, u_ref, o_ref, acc_ref, run_ref, *, t, nhs, scale):
    i = pl.program_id(2)
    d = SB_HEAD_DIM
    heads = range(nhs)
    cols = [slice(h * d, (h + 1) * d) for h in heads]
    q = [(q_ref[:, cols[h]].astype(F32) * scale).astype(BF16) for h in heads]
    u = u_ref[...]
    rep = t // LANES

    def block(j, diagonal, first):
        start = pl.multiple_of(j * t, t)
        kb = [k_ref[pl.ds(start, t), cols[h]] for h in heads]
        vb = [v_ref[pl.ds(start, t), cols[h]] for h in heads]
        z = [lax.dot_general(q[h], kb[h], _NT, preferred_element_type=F32) for h in heads]
        sp = [jnp.maximum(z[h], 0.0) + jnp.log(1.0 + jnp.exp(-jnp.abs(z[h]))) for h in heads]
        if diagonal:
            strict = (lax.broadcasted_iota(I32, (t, t), 1) < lax.broadcasted_iota(I32, (t, t), 0))
            sp = [jnp.where(strict, sp[h], 0.0) for h in heads]
        hi = [sp[h].astype(BF16) for h in heads]
        lo = [(sp[h] - hi[h].astype(F32)).astype(BF16) for h in heads]
        sums = [lax.dot_general(hi[h], u, (((1,), (0,)), ((), ())), preferred_element_type=F32)
                + lax.dot_general(lo[h], u, (((1,), (0,)), ((), ())), preferred_element_type=F32)
                for h in heads]
        if first:
            x = [z[h] + sums[h] for h in heads]
        else:
            x = [z[h] + sums[h] + _lane_tile(run_ref[h], rep) for h in heads]
        w = [jnp.exp(x[h]) for h in heads]
        if diagonal:
            w = [jnp.where(strict, w[h], 0.0) for h in heads]
        pv = [_dot(w[h], vb[h]) for h in heads]
        total = [jnp.broadcast_to(sums[h][:, 0:1], (t, LANES)) for h in heads]
        run_max = None
        for h in heads:
            if first:
                acc_ref[:, cols[h]] = pv[h]
                run_new = total[h]
            else:
                acc_ref[:, cols[h]] += pv[h]
                run_new = run_ref[h] + total[h]
            run_ref[h] = run_new
            m = jnp.max(run_new)
            run_max = m if run_max is None else jnp.maximum(run_max, m)
        return (run_max > -SB_UNDERFLOW).astype(I32)

    alive0 = block(i, True, True)

    def cond(c):
        return (c[0] <= i) & (c[1] > 0)

    def body(c):
        return c[0] + 1, block(i - c[0], False, False)

    lax.while_loop(cond, body, (jnp.int32(1), alive0))
    o_ref[...] = acc_ref[...].astype(o_ref.dtype)


def sb_attention(qkv):
    bsz, s, _ = qkv.shape
    t = min(SB_T, s)
    assert s % t == 0
    nhs = SB_HEADS_PER_STEP
    w = nhs * SB_HEAD_DIM
    ng = SB_HEADS // nhs
    r = lax.broadcasted_iota(I32, (t, t), 0)
    c = lax.broadcasted_iota(I32, (t, t), 1)
    u = jnp.where(r >= c, -1.0, 0.0).astype(BF16)
    return pl.pallas_call(
        functools.partial(_sb_attn_kernel, t=t, nhs=nhs, scale=SB_HEAD_DIM ** -0.5),
        out_shape=jax.ShapeDtypeStruct((bsz, s, SB_WIDTH), BF16),
        grid=(bsz, ng, s // t),
        in_specs=[
            pl.BlockSpec((None, t, w), lambda b, g, i: (b, i, g)),
            pl.BlockSpec((None, s, w), lambda b, g, i: (b, 0, ng + g)),
            pl.BlockSpec((None, s, w), lambda b, g, i: (b, 0, 2 * ng + g)),
            pl.BlockSpec((t, t), lambda b, g, i: (0, 0)),
        ],
        out_specs=pl.BlockSpec((None, t, w), lambda b, g, i: (b, i, g)),
        scratch_shapes=[pltpu.VMEM((t, w), F32), pltpu.VMEM((nhs, t, LANES), F32)],
        compiler_params=_cparams(("parallel", "parallel", "arbitrary")),
        name="sb_attn",
    )(qkv, qkv, qkv, u)


RWKV_TB = 2048
RWKV_GROUP = 8
RWKV_PAIR = LANES // RWKV_HEAD_DIM
RWKV_LORA_COLS = 512


def _rwkv_kernel(r_ref, k_ref, v_ref, lo_ref, rp_ref, kp_ref, vp_ref, lop_ref,
                 mur_ref, muk_ref, muv_ref, mulo_ref,
                 w0_ref, a0_ref, kk_ref, ka_ref, rk_ref, gnw_ref, gnb_ref,
                 w2_ref, a2_ref, g2_ref, o_ref, st_ref, *, tb):
    t = pl.program_id(2)
    c = CHUNK
    hd = RWKV_HEAD_DIM
    nch = range(tb // c)

    @pl.when(t == 0)
    def _():
        st_ref[...] = jnp.zeros(st_ref.shape, F32)

    li = lax.broadcasted_iota(I32, (LANES, LANES), 0)
    lj = lax.broadcasted_iota(I32, (LANES, LANES), 1)
    head_ones = (li // hd == lj // hd).astype(F32)

    def shifted(x_ref, p_ref, mu_ref):
        x = x_ref[...]
        carry = jnp.where(t == 0, 0.0, p_ref[7:8, :])
        row = lax.broadcasted_iota(I32, x.shape, 0)
        prev = jnp.where(row == 0, carry, pltpu.roll(x, 1, axis=0))
        return x + (prev - x) * mu_ref[...]

    r = shifted(r_ref, rp_ref, mur_ref)
    k = shifted(k_ref, kp_ref, muk_ref)
    v = shifted(v_ref, vp_ref, muv_ref)
    lo = shifted(lo_ref, lop_ref, mulo_ref)
    cw, ca, cg = lo[:, :LANES], lo[:, LANES:2 * LANES], lo[:, 2 * LANES:]

    lw = -RWKV_DECAY_SCALE * jax.nn.sigmoid(w0_ref[...] + _dot(jnp.tanh(cw), w2_ref[...]))
    a = jax.nn.sigmoid(a0_ref[...] + _dot(ca, a2_ref[...]))
    g = _dot(jax.nn.sigmoid(cg), g2_ref[...])

    eye = li == lj
    strict_blk = (li % c) > (lj % c)
    incl_blk = (li % c) >= (lj % c)

    kk = k * kk_ref[...]
    kk = kk * lax.rsqrt(jnp.maximum(_dot_split(kk * kk, head_ones, exact='b'), 1e-24))
    k2 = k * (1.0 + (a - 1.0) * ka_ref[...])
    av = -kk
    bv = kk * a
    bonus = _dot_split(r * k2 * rk_ref[...], head_ones, exact='b') * v

    ci = lax.broadcasted_iota(I32, (c, c), 0)
    cj = lax.broadcasted_iota(I32, (c, c), 1)
    ltri = (cj <= ci).astype(F32)
    lane = lax.broadcasted_iota(I32, (c, LANES), 1)
    head0 = lane < hd

    def stack(x):
        return jnp.concatenate([jnp.where(head0, x, 0.0), jnp.where(head0, 0.0, x)], axis=0)

    eye_f = eye.astype(F32)
    incl2 = jnp.concatenate([incl_blk, incl_blk], axis=1)
    chain = {"st": st_ref[...], "ys": [], "todo": []}

    def recur():
        if chain["todo"]:
            qm, add = chain["todo"].pop(0)
            both = _dot(qm, chain["st"]) + add
            chain["st"] = both[LANES:]
            chain["ys"].append(both[:c] + both[c:LANES])

    def factors(grp):
        rows = [slice(n * c, (n + 1) * c) for n in grp]
        ix = range(len(grp))
        cum = [_dot_split(ltri, lw[rows[n]], exact='a') for n in ix]
        recur()
        tot = [cum[n][c - 1:c, :] for n in ix]
        e_neg = [jnp.exp(-cum[n]) for n in ix]
        e_end = [jnp.exp(tot[n] - cum[n]) for n in ix]
        r_s = [stack(r[rows[n]] * jnp.exp(cum[n])) for n in ix]
        a_s = [stack(av[rows[n]] * jnp.exp(cum[n] - lw[rows[n]])) for n in ix]
        b_s = [stack(bv[rows[n]] * e_neg[n]) for n in ix]
        k_s = [stack(k2[rows[n]] * e_neg[n]) for n in ix]
        bh_s = [stack(bv[rows[n]] * e_end[n]) for n in ix]
        kh_s = [stack(k2[rows[n]] * e_end[n]) for n in ix]
        v_s = [stack(v[rows[n]]) for n in ix]
        recur()

        big = [_dot(jnp.concatenate([a_s[n], r_s[n]], axis=0),
                    jnp.concatenate([b_s[n], k_s[n]], axis=0), _NT) for n in ix]
        recur()
        nn = [jnp.where(strict_blk, big[n][:LANES, :LANES], 0.0) for n in ix]
        ak = [jnp.where(strict_blk, big[n][:LANES, LANES:], 0.0) for n in ix]
        rbk = [jnp.where(incl2, big[n][LANES:, :], 0.0) for n in ix]

        tinv = [eye_f + nn[n] for n in ix]
        npow = [_dot(nn[n], nn[n]) for n in ix]
        for _ in range(4):
            recur()
            both = [_dot(jnp.concatenate([npow[n], tinv[n]], axis=0), npow[n]) for n in ix]
            npow = [both[n][:LANES] for n in ix]
            tinv = [tinv[n] + both[n][LANES:] for n in ix]
        tinv = [tinv[n] + _dot(tinv[n], npow[n]) for n in ix]
        while chain["todo"]:
            recur()

        akv = [_dot(ak[n], v_s[n]) for n in ix]
        p12 = [_dot(tinv[n], jnp.concatenate([a_s[n], akv[n]], axis=1)) for n in ix]
        pv = [jnp.concatenate([p12[n][:, LANES:], v_s[n]], axis=0) for n in ix]
        qm_a = [_dot(jnp.concatenate([rbk[n][:, :LANES], bh_s[n].T], axis=0), p12[n][:, :LANES])
                for n in ix]
        qm_b = [_dot(jnp.concatenate([rbk[n], jnp.concatenate([bh_s[n], kh_s[n]], axis=0).T],
                                     axis=0), pv[n]) for n in ix]
        q1 = [r_s[n] + qm_a[n][:LANES] for n in ix]
        m1 = [jnp.where(eye, jnp.exp(tot[n]), 0.0) + qm_a[n][LANES:] for n in ix]
        return [(jnp.concatenate([q1[n], m1[n]], axis=0), qm_b[n]) for n in ix]

    gsz = RWKV_GROUP
    for g0 in range(0, len(nch), gsz):
        chain["todo"] = factors(list(nch)[g0:g0 + gsz])
    while chain["todo"]:
        recur()
    st_ref[...] = chain["st"]

    y = jnp.concatenate(chain["ys"], axis=0)
    gmu = _dot_split(y, head_ones, exact='b') * (1.0 / hd)
    yc = y - gmu
    var = _dot_split(yc * yc, head_ones, exact='b') * (1.0 / hd)
    yn = yc * lax.rsqrt(var + RWKV_GN_EPS) * gnw_ref[...] + gnb_ref[...]
    o_ref[...] = ((yn + bonus) * g).astype(o_ref.dtype)


def rwkv7_mixer(feat, lora, mu, mu_lora, w0, a0, k_k, k_a, r_k, gn_w, gn_b, w2p, a2p, g2):
    bsz, s, _ = feat.shape
    tb = min(RWKV_TB, s)
    assert s % tb == 0 and tb % CHUNK == 0
    w = RWKV_WIDTH
    npair = w // LANES
    nlo = RWKV_LORA_COLS
    pr = tb // 8

    def cur(width, col):
        return pl.BlockSpec((None, tb, width), lambda b, p, t: (b, t, col(p)))

    def prv(width, col):
        return pl.BlockSpec((None, 8, width),
                            lambda b, p, t: (b, jnp.maximum(t * pr - 1, 0), col(p)))

    def par(arr_cols, col):
        return pl.BlockSpec((1, arr_cols), lambda b, p, t: (0, col(p)))

    cr, ck, cv = (lambda p: p), (lambda p: npair + p), (lambda p: 2 * npair + p)
    clo = lambda p: 0
    vec = lambda x: x.reshape(1, -1)
    in_specs = [
        cur(LANES, cr), cur(LANES, ck), cur(LANES, cv), cur(nlo, clo),
        prv(LANES, cr), prv(LANES, ck), prv(LANES, cv), prv(nlo, clo),
        par(LANES, cr), par(LANES, ck), par(LANES, cv), par(nlo, clo),
    ] + [par(LANES, cr)] * 7 + [
        pl.BlockSpec((LANES, LANES), lambda b, p, t: (0, p)),
        pl.BlockSpec((LANES, LANES), lambda b, p, t: (0, p)),
        pl.BlockSpec((LORA_G, LANES), lambda b, p, t: (0, p)),
    ]
    return pl.pallas_call(
        functools.partial(_rwkv_kernel, tb=tb),
        out_shape=jax.ShapeDtypeStruct((bsz, s, w), BF16),
        grid=(bsz, npair, s // tb),
        in_specs=in_specs,
        out_specs=pl.BlockSpec((None, tb, LANES), lambda b, p, t: (b, t, p)),
        scratch_shapes=[pltpu.VMEM((LANES, LANES), F32)],
        compiler_params=_cparams(("parallel", "parallel", "arbitrary")),
        name="rwkv7",
    )(feat, feat, feat, lora, feat, feat, feat, lora,
      vec(mu), vec(mu), vec(mu), vec(mu_lora),
      vec(w0), vec(a0), vec(k_k), vec(k_a), vec(r_k), vec(gn_w), vec(gn_b),
      w2p, a2p, g2)


def _pad_cols(x, n):
    return jnp.pad(x, ((0, 0), (0, n - x.shape[1])))


def _pad_rows(x, n):
    return jnp.pad(x, ((0, n - x.shape[0]), (0, 0)))


def _rwkv_lora_cols(x):
    c0 = 3 * RWKV_WIDTH
    cw = x[..., c0:c0 + LORA_W]
    ca = x[..., c0 + LORA_W:c0 + LORA_W + LORA_A]
    cg = x[..., c0 + LORA_W + LORA_A:]
    pad = lambda y: jnp.pad(y, [(0, 0)] * (y.ndim - 1) + [(0, LANES - y.shape[-1])])
    return jnp.concatenate([pad(cw), pad(ca), cg], axis=-1)


ROUTER_COLS = LANES


def _router_params(rg, rgb, re, reb):
    d = rg.shape[0]
    w = jnp.zeros((d, ROUTER_COLS), F32).at[:, :N_GROUPS].set(rg)
    w = w.at[:, N_GROUPS:N_GROUPS + N_EXPERTS].set(re)
    b = jnp.zeros((ROUTER_COLS,), F32).at[:N_GROUPS].set(rgb)
    b = b.at[N_GROUPS:N_GROUPS + N_EXPERTS].set(reb)
    return w, b


def _tail(h_f32, mix_parts, w_out, ln1_g, ln1_b, rg, rgb, re, reb, wg, wu, wd, ln2_g, ln2_b):
    mixed = proj(mix_parts, w_out, out_dtype=BF16)
    rw, rb = _router_params(rg, rgb, re, reb)
    h1_f32, logits = ln_router(h_f32, mixed, ln1_g, ln1_b, rw, rb)
    return hier_moe_ln(h1_f32, logits, wg, wu, wd, ln2_g, ln2_b)


def kernel(x, l0_w_in, l0_shift_mu, l0_w0, l0_w2, l0_a0, l0_a2, l0_g2, l0_k_k, l0_k_a, l0_r_k, l0_gn_w, l0_gn_b, l0_w_out, l0_ln1_g, l0_ln1_b, l0_router_g, l0_router_g_b, l0_router_e, l0_router_e_b, l0_w_gate, l0_w_up, l0_w_down, l0_ln2_g, l0_ln2_b, l1_w_in, l1_lq1, l1_lk1, l1_lq2, l1_lk2, l1_subln_w, l1_gate_w2, l1_gate_b, l1_gla_norm_w, l1_w_out, l1_ln1_g, l1_ln1_b, l1_router_g, l1_router_g_b, l1_router_e, l1_router_e_b, l1_w_gate, l1_w_up, l1_w_down, l1_ln2_g, l1_ln2_b):
    bsz, s, d = x.shape
    t = bsz * s
    h0 = x.reshape(t, d)

    sbw = 3 * SB_WIDTH
    rww = 3 * RWKV_WIDTH
    x_bf16 = h0.astype(BF16)
    qkv0 = proj([x_bf16], l0_w_in, col0=0, ncols=sbw, out_dtype=BF16)
    feat = proj([x_bf16], l0_w_in, col0=sbw, ncols=rww, out_dtype=F32)
    lora = proj([x_bf16], _rwkv_lora_cols(l0_w_in[:, sbw:]), out_dtype=F32)
    o_sb = sb_attention(qkv0.reshape(bsz, s, sbw))
    o_rw = rwkv7_mixer(feat.reshape(bsz, s, rww), lora.reshape(bsz, s, RWKV_LORA_COLS),
                       l0_shift_mu[:rww], _rwkv_lora_cols(l0_shift_mu[None])[0],
                       l0_w0, l0_a0, l0_k_k, l0_k_a, l0_r_k.reshape(-1), l0_gn_w, l0_gn_b,
                       _pad_rows(l0_w2, LANES), _pad_rows(l0_a2, LANES), l0_g2)
    h2_f32, h2_bf16 = _tail(h0, [o_sb.reshape(t, SB_WIDTH), o_rw.reshape(t, RWKV_WIDTH)],
                            l0_w_out, l0_ln1_g, l0_ln1_b, l0_router_g, l0_router_g_b,
                            l0_router_e, l0_router_e_b, l0_w_gate, l0_w_up, l0_w_down,
                            l0_ln2_g, l0_ln2_b)

    dfw = 3 * DIFF_WIDTH
    glw = 2 * GLA_QK + GLA_WIDTH
    g0 = dfw + glw
    g1 = g0 + GLA_GATE_RANK
    qkv1 = proj([h2_bf16], l1_w_in, col0=0, ncols=dfw, out_dtype=BF16)
    gproj = proj([h2_bf16], l1_w_in, col0=dfw, ncols=glw, out_dtype=F32)
    og = proj([h2_bf16], l1_w_in[:, g1:], out_dtype=F32)
    cgp = proj_exact(h2_f32, _pad_cols(l1_w_in[:, g0:g1], LANES))
    lam_init = 0.8 - 0.6 * math.exp(-0.3 * 1)
    lam = (jnp.exp(jnp.sum(l1_lq1 * l1_lk1)) - jnp.exp(jnp.sum(l1_lq2 * l1_lk2)) + lam_init)
    o_diff = diff_attention(qkv1.reshape(bsz, s, dfw), lam, l1_subln_w, lam_init)
    o_gla = gla_mixer(gproj.reshape(bsz, s, glw), og.reshape(bsz, s, GLA_WIDTH),
                      cgp.reshape(bsz, s, LANES), _pad_rows(l1_gate_w2, LANES), l1_gate_b,
                      l1_gla_norm_w)
    out_f32, _ = _tail(h2_f32, [o_diff.reshape(t, DIFF_WIDTH), o_gla.reshape(t, GLA_WIDTH)],
                       l1_w_out, l1_ln1_g, l1_ln1_b, l1_router_g, l1_router_g_b,
                       l1_router_e, l1_router_e_b, l1_w_gate, l1_w_up, l1_w_down,
                       l1_ln2_g, l1_ln2_b)
    return out_f32.reshape(bsz, s, d)
```
